```python
import jax, jax.numpy as jnp
from jax import lax
import numpy as np

D_MODEL = 1024
BATCH = 8
SEQ = 8192
DEPTH = 1

CHUNK = 64
LEFT_CHUNKS = 8
BAND = (LEFT_CHUNKS + 1) * CHUNK
D_MIX = D_MODEL
D_ATTN = D_MIX // 2
D_CONV = D_MIX - D_ATTN
HEAD_DIM = 64
N_HEADS = D_ATTN // HEAD_DIM
CONV_GROUPS = 8
CONV_WIDTH = 3
MAX_REL = 128
N_REL = 2 * MAX_REL + 1
D_FF = 2816
N_SUB = 3
FFN_RES = 0.5
EPS = 1e-6
NEG_INF = -1e30

kernel_name = "hybrid_chunked_attn_shortconv_macaron_adaln"


def rms_norm(x, g):
    xf = x.astype(jnp.float32)
    y = xf * lax.rsqrt(jnp.mean(xf * xf, axis=-1, keepdims=True) + EPS)
    return (y * g.astype(jnp.float32)).astype(x.dtype)


def adaln_pre(x, g, shift, scale):
    return rms_norm(x, g) * (1.0 + scale[:, None, :]) + shift[:, None, :]


def swiglu(h, w_gu, w_down):
    gu = h @ w_gu
    g, u = jnp.split(gu, 2, axis=-1)
    return (jax.nn.silu(g) * u) @ w_down


def chunked_rel_attention(q, k, v, rel_bias):
    B, S, H, Dh = q.shape
    nc = S // CHUNK
    pad = LEFT_CHUNKS * CHUNK
    k_pad = jnp.pad(k, ((0, 0), (pad, 0), (0, 0), (0, 0)))
    v_pad = jnp.pad(v, ((0, 0), (pad, 0), (0, 0), (0, 0)))
    q_chunks = q.reshape(B, nc, CHUNK, H, Dh).transpose(1, 0, 2, 3, 4)
    a = jnp.arange(CHUNK)[:, None]
    j = jnp.arange(BAND)[None, :]
    rel_idx = jnp.clip(j - pad - a, -MAX_REL, MAX_REL) + MAX_REL
    bias = rel_bias[:, rel_idx].astype(jnp.float32)
    sm_scale = HEAD_DIM ** -0.5

    def one_chunk(args):
        i, qc = args
        start = i * CHUNK
        kb = lax.dynamic_slice_in_dim(k_pad, start, BAND, axis=1)
        vb = lax.dynamic_slice_in_dim(v_pad, start, BAND, axis=1)
        s = jnp.einsum('bqhd,bkhd->bhqk', qc, kb,
                       preferred_element_type=jnp.float32) * sm_scale + bias
        key_pos = start - pad + jnp.arange(BAND)
        s = jnp.where((key_pos >= 0)[None, None, None, :], s, NEG_INF)
        p = jax.nn.softmax(s, axis=-1)
        return jnp.einsum('bhqk,bkhd->bqhd', p.astype(vb.dtype), vb)

    out = lax.map(one_chunk, (jnp.arange(nc), q_chunks))
    return out.transpose(1, 0, 2, 3, 4).reshape(B, S, H, Dh)


def causal_depthwise_conv(u, w_conv, b_conv):
    ch = u.shape[-1]
    y = lax.conv_general_dilated(
        u, w_conv[:, None, :].astype(u.dtype), window_strides=(1,),
        padding=[(CONV_WIDTH - 1, 0)], dimension_numbers=('NWC', 'WIO', 'NWC'),
        feature_group_count=ch)
    return y + b_conv


def hybrid_mixer(h, w_in, q_norm_g, k_norm_g, rel_bias, w_conv, b_conv,
                 g_attn_out, g_conv_out, w_out):
    B, S, _ = h.shape
    proj = h @ w_in
    q, k, v, xc, gate_b, gate_c = jnp.split(
        proj, [D_ATTN, 2 * D_ATTN, 3 * D_ATTN, 3 * D_ATTN + D_CONV,
               3 * D_ATTN + 2 * D_CONV], axis=-1)
    q = rms_norm(q.reshape(B, S, N_HEADS, HEAD_DIM), q_norm_g)
    k = rms_norm(k.reshape(B, S, N_HEADS, HEAD_DIM), k_norm_g)
    v = v.reshape(B, S, N_HEADS, HEAD_DIM)
    y_attn = chunked_rel_attention(q, k, v, rel_bias).reshape(B, S, D_ATTN)
    y_conv = gate_b * causal_depthwise_conv(gate_c * xc, w_conv, b_conv)
    y = jnp.concatenate([rms_norm(y_attn, g_attn_out), rms_norm(y_conv, g_conv_out)], axis=-1)
    return y @ w_out


def setup_inputs(seed: int = 0) -> dict:
    key = jax.random.key(seed)
    ks = jax.random.split(key, 24)
    f32 = jnp.float32
    D, L = D_MODEL, DEPTH
    nrm = lambda k, shape, s: jax.random.normal(k, shape, f32) * s
    return {
        "x": nrm(ks[0], (BATCH, SEQ, D), 1.0),
        "c": nrm(ks[1], (BATCH, D), 1.0),
        "w_ada": nrm(ks[2], (L, D, N_SUB * 3 * D), 0.1 * D ** -0.5),
        "b_ada": nrm(ks[3], (L, N_SUB * 3 * D), 0.01),
        "g_norm": 1.0 + nrm(ks[4], (L, N_SUB, D), 0.02),
        "w_ffn1_in": nrm(ks[5], (L, D, 2 * D_FF), D ** -0.5),
        "w_ffn1_out": nrm(ks[6], (L, D_FF, D), D_FF ** -0.5),
        "w_in": nrm(ks[7], (L, D, 3 * D_ATTN + 3 * D_CONV), D ** -0.5),
        "q_norm_g": 1.0 + nrm(ks[8], (L, HEAD_DIM), 0.02),
        "k_norm_g": 1.0 + nrm(ks[9], (L, HEAD_DIM), 0.02),
        "rel_bias": nrm(ks[10], (L, N_HEADS, N_REL), 0.2),
        "w_conv": nrm(ks[11], (L, CONV_WIDTH, D_CONV), CONV_WIDTH ** -0.5),
        "b_conv": nrm(ks[12], (L, D_CONV), 0.01),
        "g_attn_out": 1.0 + nrm(ks[13], (L, D_ATTN), 0.02),
        "g_conv_out": 1.0 + nrm(ks[14], (L, D_CONV), 0.02),
        "w_out": nrm(ks[15], (L, D_MIX, D), D_MIX ** -0.5),
        "w_ffn2_in": nrm(ks[16], (L, D, 2 * D_FF), D ** -0.5),
        "w_ffn2_out": nrm(ks[17], (L, D_FF, D), D_FF ** -0.5),
        "g_final": 1.0 + nrm(ks[18], (L, D), 0.02),
    }


def reference(x, c, w_ada, b_ada, g_norm, w_ffn1_in, w_ffn1_out, w_in, q_norm_g,
              k_norm_g, rel_bias, w_conv, b_conv, g_attn_out, g_conv_out, w_out,
              w_ffn2_in, w_ffn2_out, g_final):
    B = x.shape[0]
    for l in range(DEPTH):
        mod = (jax.nn.silu(c) @ w_ada[l] + b_ada[l]).reshape(B, N_SUB, 3, D_MODEL)
        shift, scale, gate = mod[:, :, 0], mod[:, :, 1], mod[:, :, 2]
        h = adaln_pre(x, g_norm[l, 0], shift[:, 0], scale[:, 0])
        x = x + FFN_RES * (1.0 + gate[:, 0])[:, None, :] * swiglu(h, w_ffn1_in[l], w_ffn1_out[l])
        h = adaln_pre(x, g_norm[l, 1], shift[:, 1], scale[:, 1])
        y = hybrid_mixer(h, w_in[l], q_norm_g[l], k_norm_g[l], rel_bias[l], w_conv[l],
                         b_conv[l], g_attn_out[l], g_conv_out[l], w_out[l])
        x = x + (1.0 + gate[:, 1])[:, None, :] * y
        h = adaln_pre(x, g_norm[l, 2], shift[:, 2], scale[:, 2])
        x = x + FFN_RES * (1.0 + gate[:, 2])[:, None, :] * swiglu(h, w_ffn2_in[l], w_ffn2_out[l])
        x = rms_norm(x, g_final[l])
    return x
```

```python
import functools

import numpy as np
import jax
import jax.numpy as jnp
from jax import lax
from jax.experimental import pallas as pl
from jax.experimental.pallas import tpu as pltpu

F32 = jnp.float32
BF16 = jnp.bfloat16

CHUNK = 64
LEFT_CHUNKS = 8
HEAD_DIM = 64
CONV_WIDTH = 3
MAX_REL = 128
N_SUB = 3
FFN_RES = 0.5
EPS = 1e-6
NEG_INF = -1e30

LANES = 128
SUBLANES = 8
MXU_COL = 256

HEADS_PER_STEP = LANES // HEAD_DIM
Q_BLOCK = HEADS_PER_STEP * CHUNK
BAND = (LEFT_CHUNKS + HEADS_PER_STEP) * CHUNK
N_VARIANTS = LEFT_CHUNKS // HEADS_PER_STEP + 1
SKEW_LEN = 1024
FF_CHUNK = MXU_COL
TM_FFN = 256
TM_PROJ = 512
VMEM_LIMIT = 56 * 1024 * 1024


def _resident(block_shape, index_map):
    return pl.BlockSpec(block_shape, index_map, pipeline_mode=pl.Buffered(1))


def _params(n_axes):
    return pltpu.CompilerParams(
        dimension_semantics=("arbitrary",) * n_axes,
        vmem_limit_bytes=VMEM_LIMIT)


def _rms(x, g):
    ms = jnp.mean(x * x, axis=-1, keepdims=True)
    return x * lax.rsqrt(ms + EPS) * g


def _adaln(x, g, scale, shift):
    return _rms(x, g) * (1.0 + scale) + shift


def _mod_kernel(c_ref, w_ref, b_ref, o_ref):
    c = c_ref[...]
    sc = c * (1.0 / (1.0 + jnp.exp(-c)))
    o_ref[...] = jnp.dot(sc.astype(BF16), w_ref[...].astype(BF16),
                         preferred_element_type=F32) + b_ref[...]


def _mod(c, w_ada, b_ada):
    bsz, d = c.shape
    n = w_ada.shape[1]
    tn = 1024
    return pl.pallas_call(
        _mod_kernel,
        out_shape=jax.ShapeDtypeStruct((bsz, n), F32),
        grid=(n // tn,),
        in_specs=[pl.BlockSpec((bsz, d), lambda j: (0, 0)),
                  pl.BlockSpec((d, tn), lambda j: (0, j)),
                  pl.BlockSpec((1, tn), lambda j: (0, j))],
        out_specs=pl.BlockSpec((bsz, tn), lambda j: (0, j)),
        compiler_params=_params(1),
        name="adaln_mod",
    )(c, w_ada, b_ada.reshape(1, n))


def _ffn_kernel(x_ref, mod_ref, gn_ref, wgu_ref, wd_ref, gf_ref, o_ref, h_ref, *, sub, final):
    x = x_ref[0]
    shift = mod_ref[0, 3 * sub]
    scale = mod_ref[0, 3 * sub + 1]
    gate = mod_ref[0, 3 * sub + 2]
    h_ref[...] = _adaln(x, gn_ref[...], scale, shift).astype(BF16)
    n_chunks, _, two_tf = wgu_ref.shape
    tf = two_tf // 2
    acc = None
    for c in range(n_chunks):
        gu = jnp.dot(h_ref[...], wgu_ref[c], preferred_element_type=F32)
        g = gu[:, :tf]
        u = gu[:, tf:]
        a = (g * (1.0 / (1.0 + jnp.exp(-g))) * u).astype(BF16)
        d = jnp.dot(a, wd_ref[c], preferred_element_type=F32)
        acc = d if acc is None else acc + d
    y = x + (FFN_RES * (1.0 + gate)) * acc
    if final:
        y = _rms(y, gf_ref[...])
    o_ref[0] = y


def _ffn(x, mod, gn, wgu, wd, gf, *, sub, final):
    bsz, seq, d = x.shape
    tm = TM_FFN
    n_chunks, _, two_tf = wgu.shape
    kern = functools.partial(_ffn_kernel, sub=sub, final=final)
    return pl.pallas_call(
        kern,
        out_shape=jax.ShapeDtypeStruct((bsz, seq, d), F32),
        grid=(bsz, seq // tm),
        in_specs=[pl.BlockSpec((1, tm, d), lambda b, i: (b, i, 0)),
                  pl.BlockSpec((1, 3 * N_SUB, 1, d), lambda b, i: (b, 0, 0, 0)),
                  _resident((1, d), lambda b, i: (0, 0)),
                  _resident((n_chunks, d, two_tf), lambda b, i: (0, 0, 0)),
                  _resident((n_chunks, two_tf // 2, d), lambda b, i: (0, 0, 0)),
                  _resident((1, d), lambda b, i: (0, 0))],
        out_specs=pl.BlockSpec((1, tm, d), lambda b, i: (b, i, 0)),
        scratch_shapes=[pltpu.VMEM((tm, d), BF16)],
        compiler_params=_params(2),
        name="ffn_final" if final else "ffn",
    )(x, mod, gn, wgu, wd, gf)


def _inproj_kernel(x_ref, mod_ref, gn_ref, w_ref, bd_ref, qg_ref, kg_ref, wc_ref, bc_ref,
                   gc_ref, q_ref, k_ref, v_ref, yb_ref, h_ref, u_ref):
    i = pl.program_id(1)
    tm = x_ref.shape[1]
    da = qg_ref.shape[1]
    dc = gc_ref.shape[1]
    n_slab = da // LANES
    x = x_ref[0]
    h_ref[...] = _adaln(x, gn_ref[...], mod_ref[0, 4], mod_ref[0, 3]).astype(BF16)

    def proj(lo, width):
        return jnp.dot(h_ref[...], w_ref[:, lo:lo + width], preferred_element_type=F32)

    def head_norm(t, g):
        ss = jnp.dot((t * t).astype(BF16), bd_ref[...], preferred_element_type=F32)
        return t * lax.rsqrt(ss * (1.0 / HEAD_DIM) + EPS) * g

    qn = head_norm(proj(0, da), qg_ref[...]) * (HEAD_DIM ** -0.5)
    for p in range(n_slab):
        q_ref[0, p] = qn[:, p * LANES:(p + 1) * LANES].astype(BF16)
    kn = head_norm(proj(da, da), kg_ref[...])
    for p in range(n_slab):
        k_ref[0, p] = kn[:, p * LANES:(p + 1) * LANES].astype(BF16)
    vv = proj(2 * da, da)
    for p in range(n_slab):
        v_ref[0, p] = vv[:, p * LANES:(p + 1) * LANES].astype(BF16)

    xc = proj(3 * da, dc)
    u = proj(3 * da + 2 * dc, dc) * xc

    @pl.when(i == 0)
    def _():
        u_ref[0:SUBLANES, :] = jnp.zeros((SUBLANES, dc), F32)

    u_ref[SUBLANES:SUBLANES + tm, :] = u
    conv = (wc_ref[0:1, :] * u_ref[SUBLANES - 2:SUBLANES - 2 + tm, :]
            + wc_ref[1:2, :] * u_ref[SUBLANES - 1:SUBLANES - 1 + tm, :]
            + wc_ref[2:3, :] * u + bc_ref[...])
    y = proj(3 * da + dc, dc) * conv
    yb_ref[0] = _rms(y, gc_ref[...]).astype(BF16)
    u_ref[0:SUBLANES, :] = u_ref[tm:tm + SUBLANES, :]


def _in_proj(x, mod, gn, w_in, bd, qg, kg, wc, bc, gc):
    bsz, seq, d = x.shape
    tm = TM_PROJ
    da = qg.shape[1]
    dc = gc.shape[1]
    n_slab = da // LANES
    slab = jax.ShapeDtypeStruct((bsz, n_slab, seq, LANES), BF16)
    slab_spec = pl.BlockSpec((1, n_slab, tm, LANES), lambda b, i: (b, 0, i, 0))
    const = lambda b, i: (0, 0)
    return pl.pallas_call(
        _inproj_kernel,
        out_shape=(slab, slab, slab, jax.ShapeDtypeStruct((bsz, seq, dc), BF16)),
        grid=(bsz, seq // tm),
        in_specs=[pl.BlockSpec((1, tm, d), lambda b, i: (b, i, 0)),
                  pl.BlockSpec((1, 3 * N_SUB, 1, d), lambda b, i: (b, 0, 0, 0)),
                  _resident((1, d), const),
                  _resident(w_in.shape, const),
                  _resident(bd.shape, const),
                  _resident((1, da), const),
                  _resident((1, da), const),
                  _resident((CONV_WIDTH, dc), const),
                  _resident((1, dc), const),
                  _resident((1, dc), const)],
        out_specs=(slab_spec, slab_spec, slab_spec,
                   pl.BlockSpec((1, tm, dc), lambda b, i: (b, i, 0))),
        scratch_shapes=[pltpu.VMEM((tm, d), BF16),
                        pltpu.VMEM((tm + SUBLANES, dc), F32)],
        compiler_params=_params(2),
        name="in_proj",
    )(x, mod, gn, w_in, bd, qg, kg, wc, bc, gc)


def _attn_kernel(q_ref, k_ref, v_ref, tab_ref, o_ref):
    j = pl.program_id(2)
    n_generic = N_VARIANTS - 1
    start = pl.multiple_of(jnp.maximum(j - n_generic, 0) * Q_BLOCK, Q_BLOCK)
    variant = jnp.minimum(j, n_generic)
    q = q_ref[0, 0]
    first = lax.broadcasted_iota(jnp.int32, q.shape, 1) < HEAD_DIM
    zero = jnp.zeros_like(q)
    q2 = jnp.concatenate([jnp.where(first, q, zero), jnp.where(first, zero, q)], axis=0)
    kb = k_ref[0, 0, pl.ds(start, BAND), :]
    vb = v_ref[0, 0, pl.ds(start, BAND), :]
    s = lax.dot_general(q2, kb, (((1,), (1,)), ((), ())), preferred_element_type=F32)
    s = s + tab_ref[0, variant]
    m = jnp.max(s, axis=-1, keepdims=True)
    p = jnp.exp(s - m)
    l = jnp.sum(p, axis=-1, keepdims=True)
    o = jnp.dot(p.astype(BF16), vb, preferred_element_type=F32) / l
    o_ref[0, 0] = jnp.where(first, o[:Q_BLOCK], o[Q_BLOCK:]).astype(BF16)


def _attention(q, k, v, tab):
    bsz, n_slab, seq, _ = q.shape
    blk = pl.BlockSpec((1, 1, Q_BLOCK, LANES), lambda p, b, j: (b, p, j, 0))
    full = pl.BlockSpec((1, 1, seq, LANES), lambda p, b, j: (b, p, 0, 0))
    return pl.pallas_call(
        _attn_kernel,
        out_shape=jax.ShapeDtypeStruct(q.shape, BF16),
        grid=(n_slab, bsz, seq // Q_BLOCK),
        in_specs=[blk, full, full,
                  pl.BlockSpec((1,) + tab.shape[1:], lambda p, b, j: (p, 0, 0, 0))],
        out_specs=blk,
        compiler_params=_params(3),
        name="band_attn",
    )(q, k, v, tab)


def _bias_tables(rel_bias):
    n_heads = rel_bias.shape[0]
    pad = BAND
    edge = jnp.concatenate([jnp.repeat(rel_bias[:, :1], pad, axis=1), rel_bias,
                            jnp.repeat(rel_bias[:, -1:], pad, axis=1)], axis=1)
    neg_len = SKEW_LEN - (BAND + Q_BLOCK)
    rows = []
    for v in range(N_VARIANTS):
        s0 = pad + MAX_REL - Q_BLOCK * v
        rows.append(jnp.concatenate([edge[:, s0:s0 + SKEW_LEN - neg_len],
                                     edge[:, s0 - neg_len:s0]], axis=1))
    src = jnp.stack(rows, axis=0)
    tiled = jnp.tile(src, (1, 1, Q_BLOCK))[:, :, :Q_BLOCK * (SKEW_LEN - 1)]
    skew = tiled.reshape(N_VARIANTS, n_heads, Q_BLOCK, SKEW_LEN - 1)[..., :BAND]
    r_chunk = np.arange(Q_BLOCK)[:, None] // CHUNK
    c_chunk = np.arange(BAND)[None, :] // CHUNK
    valid = []
    for v in range(N_VARIANTS):
        q_chunk = HEADS_PER_STEP * v + r_chunk
        valid.append((c_chunk >= q_chunk - LEFT_CHUNKS) & (c_chunk <= q_chunk))
    valid = np.stack(valid)[:, None]
    tab = jnp.where(valid, skew, NEG_INF)
    tab = tab.reshape(N_VARIANTS, n_heads // HEADS_PER_STEP, HEADS_PER_STEP * Q_BLOCK, BAND)
    return tab.transpose(1, 0, 2, 3)


def _outproj_kernel(o_ref, yb_ref, x_ref, mod_ref, ga_ref, w_ref, out_ref):
    n_slab = o_ref.shape[1]
    da = n_slab * LANES
    ya = jnp.concatenate([o_ref[0, p] for p in range(n_slab)], axis=1).astype(F32)
    yan = _rms(ya, ga_ref[...]).astype(BF16)
    y = (jnp.dot(yan, w_ref[0:da, :], preferred_element_type=F32)
         + jnp.dot(yb_ref[0], w_ref[da:, :], preferred_element_type=F32))
    out_ref[0] = x_ref[0] + (1.0 + mod_ref[0, 5]) * y


def _out_proj(o, yb, x, mod, ga, w_out):
    bsz, seq, d = x.shape
    tm = TM_PROJ
    n_slab = o.shape[1]
    dc = yb.shape[2]
    const = lambda b, i: (0, 0)
    return pl.pallas_call(
        _outproj_kernel,
        out_shape=jax.ShapeDtypeStruct((bsz, seq, d), F32),
        grid=(bsz, seq // tm),
        in_specs=[pl.BlockSpec((1, n_slab, tm, LANES), lambda b, i: (b, 0, i, 0)),
                  pl.BlockSpec((1, tm, dc), lambda b, i: (b, i, 0)),
                  pl.BlockSpec((1, tm, d), lambda b, i: (b, i, 0)),
                  pl.BlockSpec((1, 3 * N_SUB, 1, d), lambda b, i: (b, 0, 0, 0)),
                  _resident((1, n_slab * LANES), const),
                  _resident(w_out.shape, const)],
        out_specs=pl.BlockSpec((1, tm, d), lambda b, i: (b, i, 0)),
        compiler_params=_params(2),
        name="out_proj",
    )(o, yb, x, mod, ga, w_out)


def _swiglu_weights(w_in, w_out):
    d, two_ff = w_in.shape
    d_ff = two_ff // 2
    n_chunks = d_ff // FF_CHUNK
    g = w_in[:, :d_ff].reshape(d, n_chunks, FF_CHUNK)
    u = w_in[:, d_ff:].reshape(d, n_chunks, FF_CHUNK)
    wgu = jnp.concatenate([g, u], axis=2).transpose(1, 0, 2).astype(BF16)
    wd = w_out.reshape(n_chunks, FF_CHUNK, d).astype(BF16)
    return wgu, wd


def kernel(x, c, w_ada, b_ada, g_norm, w_ffn1_in, w_ffn1_out, w_in, q_norm_g, k_norm_g, rel_bias,
           w_conv, b_conv, g_attn_out, g_conv_out, w_out, w_ffn2_in, w_ffn2_out, g_final):
    bsz, seq, d = x.shape
    depth = w_ada.shape[0]
    da = g_attn_out.shape[1]
    dc = g_conv_out.shape[1]
    n_heads = da // HEAD_DIM
    assert seq % TM_PROJ == 0 and seq % TM_FFN == 0 and seq % Q_BLOCK == 0
    assert da % LANES == 0 and w_in.shape[2] == 3 * da + 3 * dc
    head_id = np.arange(da) // HEAD_DIM
    bd = jnp.asarray(head_id[:, None] == head_id[None, :], dtype=BF16)
    row = lambda a: a.reshape(1, -1)
    for l in range(depth):
        mod = _mod(c, w_ada[l], b_ada[l]).reshape(bsz, 3 * N_SUB, 1, d)
        wgu1, wd1 = _swiglu_weights(w_ffn1_in[l], w_ffn1_out[l])
        wgu2, wd2 = _swiglu_weights(w_ffn2_in[l], w_ffn2_out[l])
        gf = row(g_final[l])
        x = _ffn(x, mod, row(g_norm[l, 0]), wgu1, wd1, gf, sub=0, final=False)
        q, k, v, yb = _in_proj(x, mod, row(g_norm[l, 1]), w_in[l].astype(BF16), bd,
                               row(jnp.tile(q_norm_g[l], n_heads)), row(jnp.tile(k_norm_g[l], n_heads)),
                               w_conv[l], row(b_conv[l]), row(g_conv_out[l]))
        o = _attention(q, k, v, _bias_tables(rel_bias[l]))
        x = _out_proj(o, yb, x, mod, row(g_attn_out[l]), w_out[l].astype(BF16))
        x = _ffn(x, mod, row(g_norm[l, 2]), wgu2, wd2, gf, sub=2, final=True)
    return x
```

```python
import functools

import numpy as np
import jax
import jax.numpy as jnp
from jax import lax
from jax.experimental import pallas as pl
from jax.experimental.pallas import tpu as pltpu

F32 = jnp.float32
BF16 = jnp.bfloat16

CHUNK = 64
LEFT_CHUNKS = 8
HEAD_DIM = 64
CONV_WIDTH = 3
MAX_REL = 128
N_SUB = 3
FFN_RES = 0.5
EPS = 1e-6
NEG_INF = -1e30
LOG2E = 1.4426950408889634

LANES = 128
SUBLANES = 8
BF16_SUBLANES = 16
MXU_COL = 256

HEADS_PER_SLAB = LANES // HEAD_DIM
Q_BLOCK = LANES
GROUP_BLOCKS = 2
GROUP_Q = GROUP_BLOCKS * Q_BLOCK
GROUP_COLS = GROUP_BLOCKS * HEADS_PER_SLAB * Q_BLOCK
LEFT_KEYS = LEFT_CHUNKS * CHUNK
GROUP_BAND = LEFT_KEYS + GROUP_Q
GROUP_KEY_BLOCKS = GROUP_BAND // LANES
N_START_GROUPS = LEFT_KEYS // GROUP_Q
N_VARIANTS = N_START_GROUPS + 1
SKEW_LEN = 1024
FF_CHUNK = MXU_COL
TM_FFN = 512
TM_PROJ = 512
ATTN_TICKS = 2
VMEM_LIMIT = 56 * 1024 * 1024


def _resident(block_shape, index_map):
    return pl.BlockSpec(block_shape, index_map, pipeline_mode=pl.Buffered(1))


def _params(n_axes):
    return pltpu.CompilerParams(
        dimension_semantics=("arbitrary",) * n_axes,
        vmem_limit_bytes=VMEM_LIMIT)


def _rms(x, g):
    ms = jnp.mean(x * x, axis=-1, keepdims=True)
    return x * lax.rsqrt(ms + EPS) * g


def _adaln(x, g, scale, shift):
    return _rms(x, g) * (1.0 + scale) + shift


def _mod_kernel(c_ref, w_ref, b_ref, o_ref):
    c = c_ref[...]
    sc = c * (1.0 / (1.0 + jnp.exp(-c)))
    o_ref[...] = jnp.dot(sc.astype(BF16), w_ref[...].astype(BF16),
                         preferred_element_type=F32) + b_ref[...]


def _mod(c, w_ada, b_ada):
    bsz, d = c.shape
    n = w_ada.shape[1]
    tn = 1024
    return pl.pallas_call(
        _mod_kernel,
        out_shape=jax.ShapeDtypeStruct((bsz, n), F32),
        grid=(n // tn,),
        in_specs=[pl.BlockSpec((bsz, d), lambda j: (0, 0)),
                  pl.BlockSpec((d, tn), lambda j: (0, j)),
                  pl.BlockSpec((1, tn), lambda j: (0, j))],
        out_specs=pl.BlockSpec((bsz, tn), lambda j: (0, j)),
        compiler_params=_params(1),
        name="adaln_mod",
    )(c, w_ada, b_ada.reshape(1, n))


def _ffn_kernel(x_ref, mod_ref, gn_ref, wgu_ref, wd_ref, gf_ref, o_ref, h_ref, a_ref, *, sub, final):
    x = x_ref[0]
    shift = mod_ref[0, 3 * sub]
    scale = mod_ref[0, 3 * sub + 1]
    gate = mod_ref[0, 3 * sub + 2]
    h_ref[...] = _adaln(x, gn_ref[...], scale, shift).astype(BF16)
    n_chunks, _, two_tf = wgu_ref.shape
    tf = two_tf // 2
    for c in range(n_chunks):
        gu = jnp.dot(h_ref[...], wgu_ref[c], preferred_element_type=F32)
        g = gu[:, :tf]
        u = gu[:, tf:]
        a_ref[:, c * tf:(c + 1) * tf] = (g * (1.0 / (1.0 + jnp.exp(-g))) * u).astype(BF16)
    acc = jnp.dot(a_ref[...], wd_ref[...], preferred_element_type=F32)
    y = x + (FFN_RES * (1.0 + gate)) * acc
    if final:
        y = _rms(y, gf_ref[...])
    o_ref[0] = y


def _ffn(x, mod, gn, wgu, wd, gf, *, sub, final):
    bsz, seq, d = x.shape
    tm = TM_FFN
    n_chunks, _, two_tf = wgu.shape
    kern = functools.partial(_ffn_kernel, sub=sub, final=final)
    return pl.pallas_call(
        kern,
        out_shape=jax.ShapeDtypeStruct((bsz, seq, d), F32),
        grid=(bsz, seq // tm),
        in_specs=[pl.BlockSpec((1, tm, d), lambda b, i: (b, i, 0)),
                  pl.BlockSpec((1, 3 * N_SUB, 1, d), lambda b, i: (b, 0, 0, 0)),
                  _resident((1, d), lambda b, i: (0, 0)),
                  _resident((n_chunks, d, two_tf), lambda b, i: (0, 0, 0)),
                  _resident(wd.shape, lambda b, i: (0, 0)),
                  _resident((1, d), lambda b, i: (0, 0))],
        out_specs=pl.BlockSpec((1, tm, d), lambda b, i: (b, i, 0)),
        scratch_shapes=[pltpu.VMEM((tm, d), BF16),
                        pltpu.VMEM((tm, wd.shape[0]), BF16)],
        compiler_params=_params(2),
        name="ffn_final" if final else "ffn",
    )(x, mod, gn, wgu, wd, gf)


def _inproj_kernel(x_ref, mod_ref, gn_ref, w_ref, wvt_ref, bd_ref, qg_ref, kg_ref, wc_ref, bc_ref,
                   gc_ref, q_ref, k_ref, vt_ref, yb_ref, h_ref, u_ref):
    i = pl.program_id(1)
    tm = x_ref.shape[1]
    da = qg_ref.shape[1]
    dc = gc_ref.shape[1]
    n_slab = da // LANES
    x = x_ref[0]
    h_ref[...] = _adaln(x, gn_ref[...], mod_ref[0, 4], mod_ref[0, 3]).astype(BF16)

    def proj(lo, width):
        return jnp.dot(h_ref[...], w_ref[:, lo:lo + width], preferred_element_type=F32)

    def head_norm(t, g):
        ss = jnp.dot((t * t).astype(BF16), bd_ref[...], preferred_element_type=F32)
        return t * lax.rsqrt(ss * (1.0 / HEAD_DIM) + EPS) * g

    qn = head_norm(proj(0, da), qg_ref[...]) * (HEAD_DIM ** -0.5 * LOG2E)
    for p in range(n_slab):
        q_ref[0, p] = qn[:, p * LANES:(p + 1) * LANES].astype(BF16)
    kn = head_norm(proj(da, da), kg_ref[...])
    for p in range(n_slab):
        k_ref[0, p] = kn[:, p * LANES:(p + 1) * LANES].astype(BF16)
    vt = lax.dot_general(wvt_ref[...], h_ref[...], (((1,), (1,)), ((), ())),
                         preferred_element_type=F32)
    for p in range(n_slab):
        for t in range(tm // LANES):
            vt_ref[0, p, t] = vt[p * LANES:(p + 1) * LANES, t * LANES:(t + 1) * LANES].astype(BF16)

    xc = proj(3 * da, dc)
    u = proj(3 * da + 2 * dc, dc) * xc

    @pl.when(i == 0)
    def _():
        u_ref[0:SUBLANES, :] = jnp.zeros((SUBLANES, dc), F32)

    u_ref[SUBLANES:SUBLANES + tm, :] = u
    conv = (wc_ref[0:1, :] * u_ref[SUBLANES - 2:SUBLANES - 2 + tm, :]
            + wc_ref[1:2, :] * u_ref[SUBLANES - 1:SUBLANES - 1 + tm, :]
            + wc_ref[2:3, :] * u + bc_ref[...])
    y = proj(3 * da + dc, dc) * conv
    yb_ref[0] = _rms(y, gc_ref[...]).astype(BF16)
    u_ref[0:SUBLANES, :] = u_ref[tm:tm + SUBLANES, :]


def _in_proj(x, mod, gn, w_in, wvt, bd, qg, kg, wc, bc, gc):
    bsz, seq, d = x.shape
    tm = TM_PROJ
    da = qg.shape[1]
    dc = gc.shape[1]
    n_slab = da // LANES
    slab = jax.ShapeDtypeStruct((bsz, n_slab, seq, LANES), BF16)
    slab_spec = pl.BlockSpec((1, n_slab, tm, LANES), lambda b, i: (b, 0, i, 0))
    vt_shape = jax.ShapeDtypeStruct((bsz, n_slab, seq // LANES, LANES, LANES), BF16)
    vt_spec = pl.BlockSpec((1, n_slab, tm // LANES, LANES, LANES), lambda b, i: (b, 0, i, 0, 0))
    const = lambda b, i: (0, 0)
    return pl.pallas_call(
        _inproj_kernel,
        out_shape=(slab, slab, vt_shape, jax.ShapeDtypeStruct((bsz, seq, dc), BF16)),
        grid=(bsz, seq // tm),
        in_specs=[pl.BlockSpec((1, tm, d), lambda b, i: (b, i, 0)),
                  pl.BlockSpec((1, 3 * N_SUB, 1, d), lambda b, i: (b, 0, 0, 0)),
                  _resident((1, d), const),
                  _resident(w_in.shape, const),
                  _resident(wvt.shape, const),
                  _resident(bd.shape, const),
                  _resident((1, da), const),
                  _resident((1, da), const),
                  _resident((CONV_WIDTH, dc), const),
                  _resident((1, dc), const),
                  _resident((1, dc), const)],
        out_specs=(slab_spec, slab_spec, vt_spec,
                   pl.BlockSpec((1, tm, dc), lambda b, i: (b, i, 0))),
        scratch_shapes=[pltpu.VMEM((tm, d), BF16),
                        pltpu.VMEM((tm + SUBLANES, dc), F32)],
        compiler_params=_params(2),
        name="in_proj",
    )(x, mod, gn, w_in, wvt, bd, qg, kg, wc, bc, gc)


def _attn_kernel(q_ref, k_ref, vt_ref, tab_ref, o_ref, st_a, st_b, m_a, m_b, p_a, p_b):
    n_grp = q_ref.shape[2] // GROUP_Q
    st_buf, m_buf, p_buf = (st_a, st_b), (m_a, m_b), (p_a, p_b)
    ones_rows = jnp.ones((BF16_SUBLANES, GROUP_BAND), BF16)
    lane_first = lax.broadcasted_iota(jnp.int32, (Q_BLOCK, LANES), 1) < HEAD_DIM
    row_first = lax.broadcasted_iota(jnp.int32, (LANES, Q_BLOCK), 0) < HEAD_DIM

    def first_key_block(g):
        return jnp.maximum(g - N_START_GROUPS, 0) * GROUP_BLOCKS

    def q_rows(g, blk):
        return pl.ds(pl.multiple_of(g * GROUP_Q + blk * Q_BLOCK, Q_BLOCK), Q_BLOCK)

    def scores(g, slot):
        parts = []
        for blk in range(GROUP_BLOCKS):
            q = q_ref[0, 0, q_rows(g, blk), :]
            zero = jnp.zeros_like(q)
            parts += [jnp.where(lane_first, q, zero), jnp.where(lane_first, zero, q)]
        q_all = jnp.concatenate(parts, axis=0)
        keys = pl.ds(pl.multiple_of(first_key_block(g) * LANES, LANES), GROUP_BAND)
        st = lax.dot_general(k_ref[0, 0, keys, :], q_all, (((1,), (1,)), ((), ())),
                             preferred_element_type=F32)
        st = st + tab_ref[0, jnp.minimum(g, N_START_GROUPS)]
        st_buf[slot][...] = st
        m_buf[slot][...] = jnp.max(st, axis=0, keepdims=True)

    def softmax(slot):
        p_buf[slot][...] = jnp.exp2(st_buf[slot][...] - m_buf[slot][...]).astype(BF16)

    def output(g, slot):
        kb0 = first_key_block(g)
        vt = jnp.concatenate([vt_ref[0, 0, kb0 + t] for t in range(GROUP_KEY_BLOCKS)], axis=1)
        ot = jnp.dot(jnp.concatenate([vt, ones_rows], axis=0), p_buf[slot][...],
                     preferred_element_type=F32)
        ot = ot[:LANES] * (1.0 / ot[LANES:LANES + 1])
        for blk in range(GROUP_BLOCKS):
            c0 = blk * HEADS_PER_SLAB * Q_BLOCK
            o_pair = jnp.where(row_first, ot[:, c0:c0 + Q_BLOCK], ot[:, c0 + Q_BLOCK:c0 + 2 * Q_BLOCK])
            o_ref[0, 0, q_rows(g, blk), :] = o_pair.T.astype(BF16)

    scores(0, 0)
    scores(1, 1)
    softmax(0)

    def body(i, carry):
        for t in range(ATTN_TICKS):
            g = ATTN_TICKS * i + t
            output(g, t % 2)
            scores(g + 2, t % 2)
            softmax((t + 1) % 2)
        return carry

    lax.fori_loop(0, (n_grp - 2) // ATTN_TICKS, body, 0)
    softmax(1)
    output(n_grp - 2, 0)
    output(n_grp - 1, 1)


def _attention(q, k, vt, tab):
    bsz, n_slab, seq, _ = q.shape
    assert ATTN_TICKS % 2 == 0 and (seq // GROUP_Q - 2) % ATTN_TICKS == 0
    full = pl.BlockSpec((1, 1, seq, LANES), lambda p, b: (b, p, 0, 0))
    scores = pltpu.VMEM((GROUP_BAND, GROUP_COLS), F32)
    probs = pltpu.VMEM((GROUP_BAND, GROUP_COLS), BF16)
    stat = pltpu.VMEM((1, GROUP_COLS), F32)
    return pl.pallas_call(
        _attn_kernel,
        out_shape=jax.ShapeDtypeStruct(q.shape, BF16),
        grid=(n_slab, bsz),
        in_specs=[full, full,
                  pl.BlockSpec((1, 1) + vt.shape[2:], lambda p, b: (b, p, 0, 0, 0)),
                  pl.BlockSpec((1,) + tab.shape[1:], lambda p, b: (p, 0, 0, 0))],
        out_specs=full,
        scratch_shapes=[scores, scores, stat, stat, probs, probs],
        compiler_params=_params(2),
        name="band_attn",
    )(q, k, vt, tab)


def _bias_tables(rel_bias):
    n_heads = rel_bias.shape[0]
    pad = GROUP_BAND
    edge = jnp.concatenate([jnp.repeat(rel_bias[:, :1], pad, axis=1), rel_bias,
                            jnp.repeat(rel_bias[:, -1:], pad, axis=1)], axis=1)
    neg_len = SKEW_LEN - (GROUP_BAND + Q_BLOCK)
    rows = []
    for v in range(N_VARIANTS):
        for blk in range(GROUP_BLOCKS):
            off = GROUP_Q * v + Q_BLOCK * blk
            s0 = pad + MAX_REL - off
            rows.append(jnp.concatenate([edge[:, s0:s0 + SKEW_LEN - neg_len],
                                         edge[:, s0 - neg_len:s0]], axis=1))
    src = jnp.stack(rows, axis=0)
    tiled = jnp.tile(src, (1, 1, Q_BLOCK))[:, :, :Q_BLOCK * (SKEW_LEN - 1)]
    skew = tiled.reshape(N_VARIANTS, GROUP_BLOCKS, n_heads, Q_BLOCK, SKEW_LEN - 1)[..., :GROUP_BAND]
    r_chunk = np.arange(Q_BLOCK)[:, None] // CHUNK
    c_chunk = np.arange(GROUP_BAND)[None, :] // CHUNK
    valid = np.zeros((N_VARIANTS, GROUP_BLOCKS, 1, Q_BLOCK, GROUP_BAND), bool)
    for v in range(N_VARIANTS):
        for blk in range(GROUP_BLOCKS):
            q_chunk = (GROUP_Q * v + Q_BLOCK * blk) // CHUNK + r_chunk
            valid[v, blk, 0] = (c_chunk >= q_chunk - LEFT_CHUNKS) & (c_chunk <= q_chunk)
    tab = jnp.where(valid, skew * LOG2E, NEG_INF)
    tab = tab.reshape(N_VARIANTS, GROUP_BLOCKS, n_heads // HEADS_PER_SLAB, HEADS_PER_SLAB,
                      Q_BLOCK, GROUP_BAND)
    tab = tab.transpose(2, 0, 5, 1, 3, 4)
    return tab.reshape(n_heads // HEADS_PER_SLAB, N_VARIANTS, GROUP_BAND, GROUP_COLS)


def _outproj_kernel(o_ref, yb_ref, x_ref, mod_ref, ga_ref, w_ref, out_ref):
    n_slab = o_ref.shape[1]
    da = n_slab * LANES
    ya = jnp.concatenate([o_ref[0, p] for p in range(n_slab)], axis=1).astype(F32)
    yan = _rms(ya, ga_ref[...]).astype(BF16)
    y = (jnp.dot(yan, w_ref[0:da, :], preferred_element_type=F32)
         + jnp.dot(yb_ref[0], w_ref[da:, :], preferred_element_type=F32))
    out_ref[0] = x_ref[0] + (1.0 + mod_ref[0, 5]) * y


def _out_proj(o, yb, x, mod, ga, w_out):
    bsz, seq, d = x.shape
    tm = TM_PROJ
    n_slab = o.shape[1]
    dc = yb.shape[2]
    const = lambda b, i: (0, 0)
    return pl.pallas_call(
        _outproj_kernel,
        out_shape=jax.ShapeDtypeStruct((bsz, seq, d), F32),
        grid=(bsz, seq // tm),
        in_specs=[pl.BlockSpec((1, n_slab, tm, LANES), lambda b, i: (b, 0, i, 0)),
                  pl.BlockSpec((1, tm, dc), lambda b, i: (b, i, 0)),
                  pl.BlockSpec((1, tm, d), lambda b, i: (b, i, 0)),
                  pl.BlockSpec((1, 3 * N_SUB, 1, d), lambda b, i: (b, 0, 0, 0)),
                  _resident((1, n_slab * LANES), const),
                  _resident(w_out.shape, const)],
        out_specs=pl.BlockSpec((1, tm, d), lambda b, i: (b, i, 0)),
        compiler_params=_params(2),
        name="out_proj",
    )(o, yb, x, mod, ga, w_out)


def _swiglu_weights(w_in, w_out):
    d, two_ff = w_in.shape
    d_ff = two_ff // 2
    n_chunks = d_ff // FF_CHUNK
    g = w_in[:, :d_ff].reshape(d, n_chunks, FF_CHUNK)
    u = w_in[:, d_ff:].reshape(d, n_chunks, FF_CHUNK)
    wgu = jnp.concatenate([g, u], axis=2).transpose(1, 0, 2).astype(BF16)
    return wgu, w_out.astype(BF16)


def kernel(x, c, w_ada, b_ada, g_norm, w_ffn1_in, w_ffn1_out, w_in, q_norm_g, k_norm_g, rel_bias,
           w_conv, b_conv, g_attn_out, g_conv_out, w_out, w_ffn2_in, w_ffn2_out, g_final):
    bsz, seq, d = x.shape
    depth = w_ada.shape[0]
    da = g_attn_out.shape[1]
    dc = g_conv_out.shape[1]
    n_heads = da // HEAD_DIM
    assert seq % TM_PROJ == 0 and seq % TM_FFN == 0 and seq % GROUP_Q == 0
    assert da % LANES == 0 and w_in.shape[2] == 3 * da + 3 * dc
    head_id = np.arange(da) // HEAD_DIM
    bd = jnp.asarray(head_id[:, None] == head_id[None, :], dtype=BF16)
    row = lambda a: a.reshape(1, -1)
    for l in range(depth):
        mod = _mod(c, w_ada[l], b_ada[l]).reshape(bsz, 3 * N_SUB, 1, d)
        wgu1, wd1 = _swiglu_weights(w_ffn1_in[l], w_ffn1_out[l])
        wgu2, wd2 = _swiglu_weights(w_ffn2_in[l], w_ffn2_out[l])
        gf = row(g_final[l])
        x = _ffn(x, mod, row(g_norm[l, 0]), wgu1, wd1, gf, sub=0, final=False)
        wvt = w_in[l][:, 2 * da:3 * da].T.astype(BF16)
        q, k, vt, yb = _in_proj(x, mod, row(g_norm[l, 1]), w_in[l].astype(BF16), wvt, bd,
                                row(jnp.tile(q_norm_g[l], n_heads)), row(jnp.tile(k_norm_g[l], n_heads)),
                                w_conv[l], row(b_conv[l]), row(g_conv_out[l]))
        o = _attention(q, k, vt, _bias_tables(rel_bias[l]))
        x = _out_proj(o, yb, x, mod, row(g_attn_out[l]), w_out[l].astype(BF16))
        x = _ffn(x, mod, row(g_norm[l, 2]), wgu2, wd2, gf, sub=2, final=True)
    return x
```

```python
import numpy as np
import jax
import jax.numpy as jnp
from jax import lax
from jax.experimental import pallas as pl
from jax.experimental.pallas import tpu as pltpu

F32 = jnp.float32
BF16 = jnp.bfloat16

CHUNK = 64
LEFT_CHUNKS = 8
HEAD_DIM = 64
CONV_WIDTH = 3
MAX_REL = 128
N_SUB = 3
FFN_RES = 0.5
EPS = 1e-6
NEG_INF = -1e30
LOG2E = 1.4426950408889634

LANES = 128
SUBLANES = 8
BF16_SUBLANES = 16
MXU_COL = 256

HEADS_PER_SLAB = LANES // HEAD_DIM
Q_BLOCK = LANES
GROUP_BLOCKS = 2
GROUP_Q = GROUP_BLOCKS * Q_BLOCK
GROUP_COLS = GROUP_BLOCKS * HEADS_PER_SLAB * Q_BLOCK
LEFT_KEYS = LEFT_CHUNKS * CHUNK
GROUP_BAND = LEFT_KEYS + GROUP_Q
GROUP_KEY_BLOCKS = GROUP_BAND // LANES
N_START_GROUPS = LEFT_KEYS // GROUP_Q
N_LIVE_TYPES = LEFT_KEYS // Q_BLOCK + 1
N_TYPES = N_LIVE_TYPES + 1
FF_CHUNK = MXU_COL
TM_FFN = 512
TM_PROJ = 512
ATTN_TICKS = 2
VMEM_LIMIT = 56 * 1024 * 1024


def _resident(block_shape, index_map):
    return pl.BlockSpec(block_shape, index_map, pipeline_mode=pl.Buffered(1))


def _params(n_axes):
    return pltpu.CompilerParams(
        dimension_semantics=("arbitrary",) * n_axes,
        vmem_limit_bytes=VMEM_LIMIT)


def _rms(x, g):
    ms = jnp.mean(x * x, axis=-1, keepdims=True)
    return x * lax.rsqrt(ms + EPS) * g


def _adaln(x, g, scale, shift):
    return _rms(x, g) * (1.0 + scale) + shift


def _mod_kernel(c_ref, w_ref, b_ref, o_ref):
    c = c_ref[...]
    sc = c * (1.0 / (1.0 + jnp.exp(-c)))
    o_ref[...] = jnp.dot(sc.astype(BF16), w_ref[...].astype(BF16),
                         preferred_element_type=F32) + b_ref[...]


def _mod(c, w_ada, b_ada):
    bsz, d = c.shape
    n = w_ada.shape[1]
    tn = 1024
    return pl.pallas_call(
        _mod_kernel,
        out_shape=jax.ShapeDtypeStruct((bsz, n), F32),
        grid=(n // tn,),
        in_specs=[pl.BlockSpec((bsz, d), lambda j: (0, 0)),
                  pl.BlockSpec((d, tn), lambda j: (0, j)),
                  pl.BlockSpec((1, tn), lambda j: (0, j))],
        out_specs=pl.BlockSpec((bsz, tn), lambda j: (0, j)),
        compiler_params=_params(1),
        name="adaln_mod",
    )(c, w_ada, b_ada.reshape(1, n))


def _swiglu_residual(x, shift, scale, gate, gn_ref, wgu_ref, wd_ref, h_ref, a_ref):
    h_ref[...] = _adaln(x, gn_ref[...], scale, shift).astype(BF16)
    n_chunks, _, two_tf = wgu_ref.shape
    tf = two_tf // 2
    for c in range(n_chunks):
        gu = jnp.dot(h_ref[...], wgu_ref[c], preferred_element_type=F32)
        g = gu[:, :tf]
        u = gu[:, tf:]
        a_ref[:, c * tf:(c + 1) * tf] = (g * (1.0 / (1.0 + jnp.exp(-g))) * u).astype(BF16)
    acc = jnp.dot(a_ref[...], wd_ref[...], preferred_element_type=F32)
    return x + (FFN_RES * (1.0 + gate)) * acc


def _ffn_kernel(x_ref, mod_ref, gn_ref, wgu_ref, wd_ref, o_ref, h_ref, a_ref):
    o_ref[0] = _swiglu_residual(x_ref[0], mod_ref[0, 0], mod_ref[0, 1], mod_ref[0, 2],
                                gn_ref, wgu_ref, wd_ref, h_ref, a_ref)


def _mix_ffn_kernel(o_ref, yb_ref, x_ref, mod_ref, ga_ref, wo_ref, gn_ref, wgu_ref, wd_ref, gf_ref,
                    out_ref, h_ref, a_ref):
    n_slab = o_ref.shape[1]
    da = n_slab * LANES
    ya = jnp.concatenate([o_ref[0, p] for p in range(n_slab)], axis=1).astype(F32)
    yan = _rms(ya, ga_ref[...]).astype(BF16)
    y = (jnp.dot(yan, wo_ref[0:da, :], preferred_element_type=F32)
         + jnp.dot(yb_ref[0], wo_ref[da:, :], preferred_element_type=F32))
    x = x_ref[0] + (1.0 + mod_ref[0, 5]) * y
    x = _swiglu_residual(x, mod_ref[0, 6], mod_ref[0, 7], mod_ref[0, 8],
                         gn_ref, wgu_ref, wd_ref, h_ref, a_ref)
    out_ref[0] = _rms(x, gf_ref[...])


def _ffn(x, mod, gn, wgu, wd):
    bsz, seq, d = x.shape
    tm = TM_FFN
    const2 = lambda b, i: (0, 0)
    return pl.pallas_call(
        _ffn_kernel,
        out_shape=jax.ShapeDtypeStruct((bsz, seq, d), F32),
        grid=(bsz, seq // tm),
        in_specs=[pl.BlockSpec((1, tm, d), lambda b, i: (b, i, 0)),
                  pl.BlockSpec((1, 3 * N_SUB, 1, d), lambda b, i: (b, 0, 0, 0)),
                  _resident((1, d), const2),
                  _resident(wgu.shape, lambda b, i: (0, 0, 0)),
                  _resident(wd.shape, const2)],
        out_specs=pl.BlockSpec((1, tm, d), lambda b, i: (b, i, 0)),
        scratch_shapes=[pltpu.VMEM((tm, d), BF16),
                        pltpu.VMEM((tm, wd.shape[0]), BF16)],
        compiler_params=_params(2),
        name="ffn",
    )(x, mod, gn, wgu, wd)


def _mix_ffn(o, yb, x, mod, ga, w_out, gn, wgu, wd, gf):
    bsz, seq, d = x.shape
    tm = TM_FFN
    n_slab = o.shape[1]
    dc = yb.shape[2]
    const2 = lambda b, i: (0, 0)
    return pl.pallas_call(
        _mix_ffn_kernel,
        out_shape=jax.ShapeDtypeStruct((bsz, seq, d), F32),
        grid=(bsz, seq // tm),
        in_specs=[pl.BlockSpec((1, n_slab, tm, LANES), lambda b, i: (b, 0, i, 0)),
                  pl.BlockSpec((1, tm, dc), lambda b, i: (b, i, 0)),
                  pl.BlockSpec((1, tm, d), lambda b, i: (b, i, 0)),
                  pl.BlockSpec((1, 3 * N_SUB, 1, d), lambda b, i: (b, 0, 0, 0)),
                  _resident((1, n_slab * LANES), const2),
                  _resident(w_out.shape, const2),
                  _resident((1, d), const2),
                  _resident(wgu.shape, lambda b, i: (0, 0, 0)),
                  _resident(wd.shape, const2),
                  _resident((1, d), const2)],
        out_specs=pl.BlockSpec((1, tm, d), lambda b, i: (b, i, 0)),
        scratch_shapes=[pltpu.VMEM((tm, d), BF16),
                        pltpu.VMEM((tm, wd.shape[0]), BF16)],
        compiler_params=_params(2),
        name="mix_ffn",
    )(o, yb, x, mod, ga, w_out, gn, wgu, wd, gf)


def _inproj_kernel(x_ref, mod_ref, gn_ref, w_ref, wvt_ref, bd_ref, qg_ref, kg_ref, wc_ref, bc_ref,
                   gc_ref, q_ref, k_ref, vt_ref, yb_ref, h_ref, u_ref):
    i = pl.program_id(1)
    tm = x_ref.shape[1]
    da = qg_ref.shape[1]
    dc = gc_ref.shape[1]
    n_slab = da // LANES
    x = x_ref[0]
    h_ref[...] = _adaln(x, gn_ref[...], mod_ref[0, 4], mod_ref[0, 3]).astype(BF16)

    def proj(lo, width):
        return jnp.dot(h_ref[...], w_ref[:, lo:lo + width], preferred_element_type=F32)

    def head_norm(t, g):
        ss = jnp.dot((t * t).astype(BF16), bd_ref[...], preferred_element_type=F32)
        return t * lax.rsqrt(ss * (1.0 / HEAD_DIM) + EPS) * g

    xc = proj(3 * da, dc)
    u = proj(3 * da + 2 * dc, dc) * xc

    @pl.when(i == 0)
    def _():
        u_ref[0:SUBLANES, :] = jnp.zeros((SUBLANES, dc), F32)

    u_ref[SUBLANES:SUBLANES + tm, :] = u
    conv = (wc_ref[0:1, :] * u_ref[SUBLANES - 2:SUBLANES - 2 + tm, :]
            + wc_ref[1:2, :] * u_ref[SUBLANES - 1:SUBLANES - 1 + tm, :]
            + wc_ref[2:3, :] * u + bc_ref[...])
    y = proj(3 * da + dc, dc) * conv
    yb_ref[0] = _rms(y, gc_ref[...]).astype(BF16)
    u_ref[0:SUBLANES, :] = u_ref[tm:tm + SUBLANES, :]

    qn = head_norm(proj(0, da), qg_ref[...]) * (HEAD_DIM ** -0.5 * LOG2E)
    for p in range(n_slab):
        q_ref[0, p] = qn[:, p * LANES:(p + 1) * LANES].astype(BF16)
    kn = head_norm(proj(da, da), kg_ref[...])
    for p in range(n_slab):
        k_ref[0, p] = kn[:, p * LANES:(p + 1) * LANES].astype(BF16)
    vt = lax.dot_general(wvt_ref[...], h_ref[...], (((1,), (1,)), ((), ())),
                         preferred_element_type=F32)
    for p in range(n_slab):
        for t in range(tm // LANES):
            vt_ref[0, p, t] = vt[p * LANES:(p + 1) * LANES, t * LANES:(t + 1) * LANES].astype(BF16)


def _in_proj(x, mod, gn, w_in, wvt, bd, qg, kg, wc, bc, gc):
    bsz, seq, d = x.shape
    tm = TM_PROJ
    da = qg.shape[1]
    dc = gc.shape[1]
    n_slab = da // LANES
    slab = jax.ShapeDtypeStruct((bsz, n_slab, seq, LANES), BF16)
    slab_spec = pl.BlockSpec((1, n_slab, tm, LANES), lambda b, i: (b, 0, i, 0))
    vt_shape = jax.ShapeDtypeStruct((bsz, n_slab, seq // LANES, LANES, LANES), BF16)
    vt_spec = pl.BlockSpec((1, n_slab, tm // LANES, LANES, LANES), lambda b, i: (b, 0, i, 0, 0))
    const = lambda b, i: (0, 0)
    return pl.pallas_call(
        _inproj_kernel,
        out_shape=(slab, slab, vt_shape, jax.ShapeDtypeStruct((bsz, seq, dc), BF16)),
        grid=(bsz, seq // tm),
        in_specs=[pl.BlockSpec((1, tm, d), lambda b, i: (b, i, 0)),
                  pl.BlockSpec((1, 3 * N_SUB, 1, d), lambda b, i: (b, 0, 0, 0)),
                  _resident((1, d), const),
                  _resident(w_in.shape, const),
                  _resident(wvt.shape, const),
                  _resident(bd.shape, const),
                  _resident((1, da), const),
                  _resident((1, da), const),
                  _resident((CONV_WIDTH, dc), const),
                  _resident((1, dc), const),
                  _resident((1, dc), const)],
        out_specs=(slab_spec, slab_spec, vt_spec,
                   pl.BlockSpec((1, tm, dc), lambda b, i: (b, i, 0))),
        scratch_shapes=[pltpu.VMEM((tm, d), BF16),
                        pltpu.VMEM((tm + SUBLANES, dc), F32)],
        compiler_params=_params(2),
        name="in_proj",
    )(x, mod, gn, w_in, wvt, bd, qg, kg, wc, bc, gc)


def _attn_kernel(q_ref, k_ref, vt_ref, tab_ref, o_ref, st_a, st_b, m_a, m_b, p_a, p_b):
    n_grp = q_ref.shape[2] // GROUP_Q
    st_buf, m_buf, p_buf = (st_a, st_b), (m_a, m_b), (p_a, p_b)
    ones_rows = jnp.ones((BF16_SUBLANES, GROUP_BAND), BF16)
    lane_first = lax.broadcasted_iota(jnp.int32, (Q_BLOCK, LANES), 1) < HEAD_DIM
    row_first = lax.broadcasted_iota(jnp.int32, (LANES, Q_BLOCK), 0) < HEAD_DIM

    def first_key_block(g):
        return jnp.maximum(g - N_START_GROUPS, 0) * GROUP_BLOCKS

    def q_rows(g, blk):
        return pl.ds(pl.multiple_of(g * GROUP_Q + blk * Q_BLOCK, Q_BLOCK), Q_BLOCK)

    def table(g):
        lag0 = GROUP_BLOCKS * jnp.minimum(g, N_START_GROUPS) - (N_LIVE_TYPES - 1)
        rows = []
        for cb in range(GROUP_KEY_BLOCKS):
            cols = []
            for blk in range(GROUP_BLOCKS):
                t = cb - blk - lag0
                t = jnp.where((t >= 0) & (t < N_LIVE_TYPES), t, N_LIVE_TYPES)
                cols += [tab_ref[0, t, h] for h in range(HEADS_PER_SLAB)]
            rows.append(jnp.concatenate(cols, axis=1))
        return jnp.concatenate(rows, axis=0)

    def scores(g, slot):
        parts = []
        for blk in range(GROUP_BLOCKS):
            q = q_ref[0, 0, q_rows(g, blk), :]
            zero = jnp.zeros_like(q)
            parts += [jnp.where(lane_first, q, zero), jnp.where(lane_first, zero, q)]
        q_all = jnp.concatenate(parts, axis=0)
        keys = pl.ds(pl.multiple_of(first_key_block(g) * LANES, LANES), GROUP_BAND)
        st = lax.dot_general(k_ref[0, 0, keys, :], q_all, (((1,), (1,)), ((), ())),
                             preferred_element_type=F32)
        st = st + table(g)
        st_buf[slot][...] = st
        m_buf[slot][...] = jnp.max(st, axis=0, keepdims=True)

    def softmax(slot):
        p_buf[slot][...] = jnp.exp2(st_buf[slot][...] - m_buf[slot][...]).astype(BF16)

    def output(g, slot):
        kb0 = first_key_block(g)
        vt = jnp.concatenate([vt_ref[0, 0, kb0 + t] for t in range(GROUP_KEY_BLOCKS)], axis=1)
        ot = jnp.dot(jnp.concatenate([vt, ones_rows], axis=0), p_buf[slot][...],
                     preferred_element_type=F32)
        ot = ot[:LANES] * (1.0 / ot[LANES:LANES + 1])
        for blk in range(GROUP_BLOCKS):
            c0 = blk * HEADS_PER_SLAB * Q_BLOCK
            o_pair = jnp.where(row_first, ot[:, c0:c0 + Q_BLOCK], ot[:, c0 + Q_BLOCK:c0 + 2 * Q_BLOCK])
            o_ref[0, 0, q_rows(g, blk), :] = o_pair.T.astype(BF16)

    scores(0, 0)
    scores(1, 1)
    softmax(0)

    def body(i, carry):
        for t in range(ATTN_TICKS):
            g = ATTN_TICKS * i + t
            output(g, t % 2)
            scores(g + 2, t % 2)
            softmax((t + 1) % 2)
        return carry

    lax.fori_loop(0, (n_grp - 2) // ATTN_TICKS, body, 0)
    softmax(1)
    output(n_grp - 2, 0)
    output(n_grp - 1, 1)


def _attention(q, k, vt, tab):
    bsz, n_slab, seq, _ = q.shape
    assert ATTN_TICKS % 2 == 0 and (seq // GROUP_Q - 2) % ATTN_TICKS == 0
    full = pl.BlockSpec((1, 1, seq, LANES), lambda p, b: (b, p, 0, 0))
    scores = pltpu.VMEM((GROUP_BAND, GROUP_COLS), F32)
    probs = pltpu.VMEM((GROUP_BAND, GROUP_COLS), BF16)
    stat = pltpu.VMEM((1, GROUP_COLS), F32)
    return pl.pallas_call(
        _attn_kernel,
        out_shape=jax.ShapeDtypeStruct(q.shape, BF16),
        grid=(n_slab, bsz),
        in_specs=[full, full,
                  pl.BlockSpec((1, 1) + vt.shape[2:], lambda p, b: (b, p, 0, 0, 0)),
                  pl.BlockSpec((1,) + tab.shape[1:], lambda p, b: (p, 0, 0, 0, 0))],
        out_specs=full,
        scratch_shapes=[scores, scores, stat, stat, probs, probs],
        compiler_params=_params(2),
        name="band_attn",
    )(q, k, vt, tab)


def _bias_tables(rel_bias):
    n_heads = rel_bias.shape[0]
    pad_left = LEFT_KEYS
    edge = jnp.concatenate([jnp.repeat(rel_bias[:, :1], pad_left, axis=1), rel_bias], axis=1)
    rev = edge[:, ::-1]
    period = 2 * Q_BLOCK
    rows = []
    for t in range(N_LIVE_TYPES):
        base = pad_left + MAX_REL + Q_BLOCK * (t - (N_LIVE_TYPES - 1))
        q0 = edge.shape[1] - 1 - base
        rows.append(jnp.concatenate([rev[:, q0:q0 + Q_BLOCK], rev[:, q0 - Q_BLOCK:q0]], axis=1))
    src = jnp.stack(rows, axis=0)
    tiled = jnp.tile(src, (1, 1, Q_BLOCK))[:, :, :Q_BLOCK * (period - 1)]
    skew = tiled.reshape(N_LIVE_TYPES, n_heads, Q_BLOCK, period - 1)[..., :Q_BLOCK]
    key_chunk = np.arange(Q_BLOCK)[:, None] // CHUNK
    qry_chunk = np.arange(Q_BLOCK)[None, :] // CHUNK
    lag = np.stack([2 * (t - (N_LIVE_TYPES - 1)) + key_chunk - qry_chunk for t in range(N_LIVE_TYPES)])
    valid = ((lag >= -LEFT_CHUNKS) & (lag <= 0))[:, None]
    live = jnp.where(valid, skew * LOG2E, NEG_INF)
    tab = jnp.concatenate([live, jnp.full((1,) + live.shape[1:], NEG_INF, F32)], axis=0)
    tab = tab.reshape(N_TYPES, n_heads // HEADS_PER_SLAB, HEADS_PER_SLAB, Q_BLOCK, Q_BLOCK)
    return tab.transpose(1, 0, 2, 3, 4)


def _swiglu_weights(w_in, w_out):
    d, two_ff = w_in.shape
    d_ff = two_ff // 2
    n_chunks = d_ff // FF_CHUNK
    g = w_in[:, :d_ff].reshape(d, n_chunks, FF_CHUNK)
    u = w_in[:, d_ff:].reshape(d, n_chunks, FF_CHUNK)
    wgu = jnp.concatenate([g, u], axis=2).transpose(1, 0, 2).astype(BF16)
    return wgu, w_out.astype(BF16)


def kernel(x, c, w_ada, b_ada, g_norm, w_ffn1_in, w_ffn1_out, w_in, q_norm_g, k_norm_g, rel_bias,
           w_conv, b_conv, g_attn_out, g_conv_out, w_out, w_ffn2_in, w_ffn2_out, g_final):
    bsz, seq, d = x.shape
    depth = w_ada.shape[0]
    da = g_attn_out.shape[1]
    dc = g_conv_out.shape[1]
    n_heads = da // HEAD_DIM
    assert seq % TM_PROJ == 0 and seq % TM_FFN == 0 and seq % GROUP_Q == 0
    assert da % LANES == 0 and w_in.shape[2] == 3 * da + 3 * dc
    head_id = np.arange(da) // HEAD_DIM
    bd = jnp.asarray(head_id[:, None] == head_id[None, :], dtype=BF16)
    row = lambda a: a.reshape(1, -1)
    for l in range(depth):
        mod = _mod(c, w_ada[l], b_ada[l]).reshape(bsz, 3 * N_SUB, 1, d)
        wgu1, wd1 = _swiglu_weights(w_ffn1_in[l], w_ffn1_out[l])
        wgu2, wd2 = _swiglu_weights(w_ffn2_in[l], w_ffn2_out[l])
        x = _ffn(x, mod, row(g_norm[l, 0]), wgu1, wd1)
        wvt = w_in[l][:, 2 * da:3 * da].T.astype(BF16)
        q, k, vt, yb = _in_proj(x, mod, row(g_norm[l, 1]), w_in[l].astype(BF16), wvt, bd,
                                row(jnp.tile(q_norm_g[l], n_heads)), row(jnp.tile(k_norm_g[l], n_heads)),
                                w_conv[l], row(b_conv[l]), row(g_conv_out[l]))
        o = _attention(q, k, vt, _bias_tables(rel_bias[l]))
        x = _mix_ffn(o, yb, x, mod, row(g_attn_out[l]), w_out[l].astype(BF16),
                     row(g_norm[l, 2]), wgu2, wd2, row(g_final[l]))
    return x
```

```python
import numpy as np
import jax
import jax.numpy as jnp
from jax import lax
from jax.experimental import pallas as pl
from jax.experimental.pallas import tpu as pltpu

F32 = jnp.float32
BF16 = jnp.bfloat16

CHUNK = 64
LEFT_CHUNKS = 8
HEAD_DIM = 64
CONV_WIDTH = 3
MAX_REL = 128
N_SUB = 3
FFN_RES = 0.5
EPS = 1e-6
NEG_INF = -1e30
LOG2E = 1.4426950408889634

LANES = 128
SUBLANES = 8
BF16_SUBLANES = 16
MXU_COL = 256

HEADS_PER_SLAB = LANES // HEAD_DIM
Q_BLOCK = LANES
GROUP_BLOCKS = 1
GROUP_Q = GROUP_BLOCKS * Q_BLOCK
GROUP_COLS = GROUP_BLOCKS * HEADS_PER_SLAB * Q_BLOCK
LEFT_KEYS = LEFT_CHUNKS * CHUNK
GROUP_BAND = LEFT_KEYS + GROUP_Q
GROUP_KEY_BLOCKS = GROUP_BAND // LANES
N_START_GROUPS = LEFT_KEYS // GROUP_Q
N_LIVE_TYPES = LEFT_KEYS // Q_BLOCK + 1
N_TYPES = N_LIVE_TYPES + 1
FF_CHUNK = MXU_COL
TM_FFN = 1024
FFN_SPLIT = 4
TM_PROJ = 1024
PROJ_SPLIT = 4
PIPE_DIST = 8
ATTN_SLOTS = 2 * PIPE_DIST
VMEM_LIMIT = 56 * 1024 * 1024


def _resident(block_shape, index_map):
    return pl.BlockSpec(block_shape, index_map, pipeline_mode=pl.Buffered(1))


def _params(n_axes):
    return pltpu.CompilerParams(
        dimension_semantics=("arbitrary",) * n_axes,
        vmem_limit_bytes=VMEM_LIMIT)


def _rms(x, g):
    ms = jnp.mean(x * x, axis=-1, keepdims=True)
    return x * lax.rsqrt(ms + EPS) * g


def _adaln(x, g, scale, shift):
    return _rms(x, g) * (1.0 + scale) + shift


def _mod_kernel(c_ref, w_ref, b_ref, o_ref):
    c = c_ref[...]
    sc = c * (1.0 / (1.0 + jnp.exp(-c)))
    o_ref[...] = jnp.dot(sc.astype(BF16), w_ref[...].astype(BF16),
                         preferred_element_type=F32) + b_ref[...]


def _mod(c, w_ada, b_ada):
    bsz, d = c.shape
    n = w_ada.shape[1]
    tn = 1024
    return pl.pallas_call(
        _mod_kernel,
        out_shape=jax.ShapeDtypeStruct((bsz, n), F32),
        grid=(n // tn,),
        in_specs=[pl.BlockSpec((bsz, d), lambda j: (0, 0)),
                  pl.BlockSpec((d, tn), lambda j: (0, j)),
                  pl.BlockSpec((1, tn), lambda j: (0, j))],
        out_specs=pl.BlockSpec((bsz, tn), lambda j: (0, j)),
        compiler_params=_params(1),
        name="adaln_mod",
    )(c, w_ada, b_ada.reshape(1, n))


def _swiglu_residual(xs, shift, scale, gate, gn_ref, wgu_ref, wd_ref, h_ref, a_ref):
    rows = xs[0].shape[0]
    blk = [slice(i * rows, (i + 1) * rows) for i in range(len(xs))]
    for r, x in zip(blk, xs):
        h_ref[r, :] = _adaln(x, gn_ref[...], scale, shift).astype(BF16)
    d_ff = wd_ref.shape[0]
    tf = FF_CHUNK
    for r in blk:
        for c in range(d_ff // tf):
            w_c = jnp.concatenate([wgu_ref[:, c * tf:(c + 1) * tf],
                                   wgu_ref[:, d_ff + c * tf:d_ff + (c + 1) * tf]], axis=1)
            gu = jnp.dot(h_ref[r, :], w_c, preferred_element_type=F32)
            g = gu[:, :tf]
            u = gu[:, tf:]
            a_ref[r, c * tf:(c + 1) * tf] = (g * (1.0 / (1.0 + jnp.exp(-g))) * u).astype(BF16)
    outs = []
    for r, x in zip(blk, xs):
        acc = jnp.dot(a_ref[r, :], wd_ref[...], preferred_element_type=F32)
        outs.append(x + (FFN_RES * (1.0 + gate)) * acc)
    return outs


def _row_blocks(n_rows):
    rows = n_rows // FFN_SPLIT
    return [slice(i * rows, (i + 1) * rows) for i in range(FFN_SPLIT)]


def _ffn_kernel(x_ref, mod_ref, gn_ref, wgu_ref, wd_ref, o_ref, h_ref, a_ref):
    blk = _row_blocks(x_ref.shape[1])
    ys = _swiglu_residual([x_ref[0, r, :] for r in blk], mod_ref[0, 0], mod_ref[0, 1], mod_ref[0, 2],
                          gn_ref, wgu_ref, wd_ref, h_ref, a_ref)
    for r, y in zip(blk, ys):
        o_ref[0, r, :] = y


def _mix_ffn_kernel(o_ref, yb_ref, x_ref, mod_ref, ga_ref, wo_ref, gn_ref, wgu_ref, wd_ref, gf_ref,
                    out_ref, h_ref, a_ref):
    n_slab = o_ref.shape[1]
    da = n_slab * LANES
    blk = _row_blocks(x_ref.shape[1])
    xs = []
    for r in blk:
        ya = jnp.concatenate([o_ref[0, p, r, :] for p in range(n_slab)], axis=1).astype(F32)
        yan = _rms(ya, ga_ref[...]).astype(BF16)
        y = (jnp.dot(yan, wo_ref[0:da, :], preferred_element_type=F32)
             + jnp.dot(yb_ref[0, r, :], wo_ref[da:, :], preferred_element_type=F32))
        xs.append(x_ref[0, r, :] + (1.0 + mod_ref[0, 5]) * y)
    ys = _swiglu_residual(xs, mod_ref[0, 6], mod_ref[0, 7], mod_ref[0, 8],
                          gn_ref, wgu_ref, wd_ref, h_ref, a_ref)
    for r, y in zip(blk, ys):
        out_ref[0, r, :] = _rms(y, gf_ref[...])


def _ffn(x, mod, gn, wgu, wd):
    bsz, seq, d = x.shape
    tm = TM_FFN
    const2 = lambda b, i: (0, 0)
    return pl.pallas_call(
        _ffn_kernel,
        out_shape=jax.ShapeDtypeStruct((bsz, seq, d), F32),
        grid=(bsz, seq // tm),
        in_specs=[pl.BlockSpec((1, tm, d), lambda b, i: (b, i, 0)),
                  pl.BlockSpec((1, 3 * N_SUB, 1, d), lambda b, i: (b, 0, 0, 0)),
                  _resident((1, d), const2),
                  _resident(wgu.shape, const2),
                  _resident(wd.shape, const2)],
        out_specs=pl.BlockSpec((1, tm, d), lambda b, i: (b, i, 0)),
        scratch_shapes=[pltpu.VMEM((tm, d), BF16),
                        pltpu.VMEM((tm, wd.shape[0]), BF16)],
        compiler_params=_params(2),
        name="ffn",
    )(x, mod, gn, wgu, wd)


def _mix_ffn(o, yb, x, mod, ga, w_out, gn, wgu, wd, gf):
    bsz, seq, d = x.shape
    tm = TM_FFN
    n_slab = o.shape[1]
    dc = yb.shape[2]
    const2 = lambda b, i: (0, 0)
    return pl.pallas_call(
        _mix_ffn_kernel,
        out_shape=jax.ShapeDtypeStruct((bsz, seq, d), F32),
        grid=(bsz, seq // tm),
        in_specs=[pl.BlockSpec((1, n_slab, tm, LANES), lambda b, i: (b, 0, i, 0)),
                  pl.BlockSpec((1, tm, dc), lambda b, i: (b, i, 0)),
                  pl.BlockSpec((1, tm, d), lambda b, i: (b, i, 0)),
                  pl.BlockSpec((1, 3 * N_SUB, 1, d), lambda b, i: (b, 0, 0, 0)),
                  _resident((1, n_slab * LANES), const2),
                  _resident(w_out.shape, const2),
                  _resident((1, d), const2),
                  _resident(wgu.shape, const2),
                  _resident(wd.shape, const2),
                  _resident((1, d), const2)],
        out_specs=pl.BlockSpec((1, tm, d), lambda b, i: (b, i, 0)),
        scratch_shapes=[pltpu.VMEM((tm, d), BF16),
                        pltpu.VMEM((tm, wd.shape[0]), BF16)],
        compiler_params=_params(2),
        name="mix_ffn",
    )(o, yb, x, mod, ga, w_out, gn, wgu, wd, gf)


def _inproj_kernel(x_ref, mod_ref, gn_ref, w_ref, wvt_ref, qg_ref, kg_ref, wc_ref, bc_ref,
                   gc_ref, q_ref, k_ref, vt_ref, yb_ref, h_ref, u_ref):
    i = pl.program_id(1)
    tm = x_ref.shape[1]
    da = qg_ref.shape[1]
    dc = gc_ref.shape[1]
    n_slab = da // LANES
    rows = tm // PROJ_SPLIT
    lane = lax.broadcasted_iota(jnp.int32, (1, LANES), 1)

    def head_sumsq(t):
        slabs = []
        for p in range(n_slab):
            sq = t[:, p * LANES:(p + 1) * LANES]
            sq = sq * sq
            acc = None
            for hd in range(HEADS_PER_SLAB):
                mine = (lane >= hd * HEAD_DIM) & (lane < (hd + 1) * HEAD_DIM)
                tot = jnp.sum(jnp.where(mine, sq, 0.0), axis=-1, keepdims=True)
                acc = jnp.where(mine, tot, 0.0 if acc is None else acc)
            slabs.append(acc)
        return jnp.concatenate(slabs, axis=1)

    def head_norm(t, ss, g):
        return t * lax.rsqrt(ss * (1.0 / HEAD_DIM) + EPS) * g

    @pl.when(i == 0)
    def _():
        u_ref[0:SUBLANES, :] = jnp.zeros((SUBLANES, dc), F32)

    for b in range(PROJ_SPLIT):
        r = slice(b * rows, (b + 1) * rows)
        h_ref[r, :] = _adaln(x_ref[0, r, :], gn_ref[...], mod_ref[0, 4], mod_ref[0, 3]).astype(BF16)

        def proj(lo, width):
            return jnp.dot(h_ref[r, :], w_ref[:, lo:lo + width], preferred_element_type=F32)

        xc = proj(3 * da, dc)
        gate_c = proj(3 * da + 2 * dc, dc)
        gate_b = proj(3 * da + dc, dc)
        u = gate_c * xc
        base = SUBLANES + b * rows
        u_ref[base:base + rows, :] = u
        conv = (wc_ref[0:1, :] * u_ref[base - 2:base - 2 + rows, :]
                + wc_ref[1:2, :] * u_ref[base - 1:base - 1 + rows, :]
                + wc_ref[2:3, :] * u + bc_ref[...])
        yb_ref[0, r, :] = _rms(gate_b * conv, gc_ref[...]).astype(BF16)

        tq = proj(0, da)
        tk = proj(da, da)
        vt = lax.dot_general(wvt_ref[...], h_ref[r, :], (((1,), (1,)), ((), ())),
                             preferred_element_type=F32)
        qn = head_norm(tq, head_sumsq(tq), qg_ref[...]) * (HEAD_DIM ** -0.5 * LOG2E)
        kn = head_norm(tk, head_sumsq(tk), kg_ref[...])
        for p in range(n_slab):
            q_ref[0, p, r, :] = qn[:, p * LANES:(p + 1) * LANES].astype(BF16)
            k_ref[0, p, r, :] = kn[:, p * LANES:(p + 1) * LANES].astype(BF16)
            for t in range(rows // LANES):
                vt_ref[0, p, b * (rows // LANES) + t] = (
                    vt[p * LANES:(p + 1) * LANES, t * LANES:(t + 1) * LANES].astype(BF16))

    u_ref[0:SUBLANES, :] = u_ref[tm:tm + SUBLANES, :]


def _in_proj(x, mod, gn, w_in, wvt, qg, kg, wc, bc, gc):
    bsz, seq, d = x.shape
    tm = TM_PROJ
    da = qg.shape[1]
    dc = gc.shape[1]
    n_slab = da // LANES
    slab = jax.ShapeDtypeStruct((bsz, n_slab, seq, LANES), BF16)
    slab_spec = pl.BlockSpec((1, n_slab, tm, LANES), lambda b, i: (b, 0, i, 0))
    vt_shape = jax.ShapeDtypeStruct((bsz, n_slab, seq // LANES, LANES, LANES), BF16)
    vt_spec = pl.BlockSpec((1, n_slab, tm // LANES, LANES, LANES), lambda b, i: (b, 0, i, 0, 0))
    const = lambda b, i: (0, 0)
    return pl.pallas_call(
        _inproj_kernel,
        out_shape=(slab, slab, vt_shape, jax.ShapeDtypeStruct((bsz, seq, dc), BF16)),
        grid=(bsz, seq // tm),
        in_specs=[pl.BlockSpec((1, tm, d), lambda b, i: (b, i, 0)),
                  pl.BlockSpec((1, 3 * N_SUB, 1, d), lambda b, i: (b, 0, 0, 0)),
                  _resident((1, d), const),
                  _resident(w_in.shape, const),
                  _resident(wvt.shape, const),
                  _resident((1, da), const),
                  _resident((1, da), const),
                  _resident((CONV_WIDTH, dc), const),
                  _resident((1, dc), const),
                  _resident((1, dc), const)],
        out_specs=(slab_spec, slab_spec, vt_spec,
                   pl.BlockSpec((1, tm, dc), lambda b, i: (b, i, 0))),
        scratch_shapes=[pltpu.VMEM((tm, d), BF16),
                        pltpu.VMEM((tm + SUBLANES, dc), F32)],
        compiler_params=_params(2),
        name="in_proj",
    )(x, mod, gn, w_in, wvt, qg, kg, wc, bc, gc)


def _attn_kernel(q_ref, k_ref, vt_ref, tab_ref, o_ref, *scratch):
    n_grp = q_ref.shape[2] // GROUP_Q
    st_buf = scratch[:ATTN_SLOTS]
    m_buf = scratch[ATTN_SLOTS:2 * ATTN_SLOTS]
    p_buf = scratch[2 * ATTN_SLOTS:]
    ones_rows = jnp.ones((BF16_SUBLANES, GROUP_BAND), BF16)
    lane_first = lax.broadcasted_iota(jnp.int32, (Q_BLOCK, LANES), 1) < HEAD_DIM
    row_first = lax.broadcasted_iota(jnp.int32, (LANES, Q_BLOCK), 0) < HEAD_DIM

    def first_key_block(g):
        return jnp.maximum(g - N_START_GROUPS, 0) * GROUP_BLOCKS

    def q_rows(g, blk):
        return pl.ds(pl.multiple_of(g * GROUP_Q + blk * Q_BLOCK, Q_BLOCK), Q_BLOCK)

    def table(g):
        lag0 = GROUP_BLOCKS * jnp.minimum(g, N_START_GROUPS) - (N_LIVE_TYPES - 1)
        rows = []
        for cb in range(GROUP_KEY_BLOCKS):
            cols = []
            for blk in range(GROUP_BLOCKS):
                t = cb - blk - lag0
                t = jnp.where((t >= 0) & (t < N_LIVE_TYPES), t, N_LIVE_TYPES)
                cols += [tab_ref[0, t, h] for h in range(HEADS_PER_SLAB)]
            rows.append(jnp.concatenate(cols, axis=1))
        return jnp.concatenate(rows, axis=0)

    def scores(g, slot):
        parts = []
        for blk in range(GROUP_BLOCKS):
            q = q_ref[0, 0, q_rows(g, blk), :]
            zero = jnp.zeros_like(q)
            parts += [jnp.where(lane_first, q, zero), jnp.where(lane_first, zero, q)]
        q_all = jnp.concatenate(parts, axis=0)
        keys = pl.ds(pl.multiple_of(first_key_block(g) * LANES, LANES), GROUP_BAND)
        st = lax.dot_general(k_ref[0, 0, keys, :], q_all, (((1,), (1,)), ((), ())),
                             preferred_element_type=F32)
        st = st + table(g)
        st_buf[slot][...] = st
        m_buf[slot][...] = jnp.max(st, axis=0, keepdims=True)

    def softmax(slot):
        p_buf[slot][...] = jnp.exp2(st_buf[slot][...] - m_buf[slot][...]).astype(BF16)

    def output(g, slot):
        kb0 = first_key_block(g)
        vt = jnp.concatenate([vt_ref[0, 0, kb0 + t] for t in range(GROUP_KEY_BLOCKS)], axis=1)
        ot = jnp.dot(jnp.concatenate([ones_rows, vt], axis=0), p_buf[slot][...],
                     preferred_element_type=F32)
        ot = ot[BF16_SUBLANES:] * (1.0 / ot[0:1])
        for blk in range(GROUP_BLOCKS):
            c0 = blk * HEADS_PER_SLAB * Q_BLOCK
            o_pair = jnp.where(row_first, ot[:, c0:c0 + Q_BLOCK], ot[:, c0 + Q_BLOCK:c0 + 2 * Q_BLOCK])
            o_ref[0, 0, q_rows(g, blk), :] = o_pair.T.astype(BF16)

    dist = PIPE_DIST
    for g in range(2 * dist):
        scores(g, g % ATTN_SLOTS)
    for g in range(dist):
        softmax(g % ATTN_SLOTS)

    def body(i, carry):
        for t in range(ATTN_SLOTS):
            g = ATTN_SLOTS * i + t
            output(g, t)
            scores(g + 2 * dist, t)
            softmax((t + dist) % ATTN_SLOTS)
        return carry

    lax.fori_loop(0, (n_grp - 2 * dist) // ATTN_SLOTS, body, 0)
    for g in range(n_grp - 2 * dist, n_grp):
        output(g, g % ATTN_SLOTS)
        if g + dist < n_grp:
            softmax((g + dist) % ATTN_SLOTS)


def _attention(q, k, vt, tab):
    bsz, n_slab, seq, _ = q.shape
    assert (seq // GROUP_Q - 2 * PIPE_DIST) % ATTN_SLOTS == 0
    full = pl.BlockSpec((1, 1, seq, LANES), lambda p, b: (b, p, 0, 0))
    scores = pltpu.VMEM((GROUP_BAND, GROUP_COLS), F32)
    probs = pltpu.VMEM((GROUP_BAND, GROUP_COLS), BF16)
    stat = pltpu.VMEM((1, GROUP_COLS), F32)
    return pl.pallas_call(
        _attn_kernel,
        out_shape=jax.ShapeDtypeStruct(q.shape, BF16),
        grid=(n_slab, bsz),
        in_specs=[full, full,
                  pl.BlockSpec((1, 1) + vt.shape[2:], lambda p, b: (b, p, 0, 0, 0)),
                  pl.BlockSpec((1,) + tab.shape[1:], lambda p, b: (p, 0, 0, 0, 0))],
        out_specs=full,
        scratch_shapes=[scores] * ATTN_SLOTS + [stat] * ATTN_SLOTS + [probs] * ATTN_SLOTS,
        compiler_params=_params(2),
        name="band_attn",
    )(q, k, vt, tab)


def _bias_tables(rel_bias):
    n_heads = rel_bias.shape[0]
    pad_left = LEFT_KEYS
    edge = jnp.concatenate([jnp.repeat(rel_bias[:, :1], pad_left, axis=1), rel_bias], axis=1)
    rev = edge[:, ::-1]
    period = 2 * Q_BLOCK
    rows = []
    for t in range(N_LIVE_TYPES):
        base = pad_left + MAX_REL + Q_BLOCK * (t - (N_LIVE_TYPES - 1))
        q0 = edge.shape[1] - 1 - base
        rows.append(jnp.concatenate([rev[:, q0:q0 + Q_BLOCK], rev[:, q0 - Q_BLOCK:q0]], axis=1))
    src = jnp.stack(rows, axis=0)
    tiled = jnp.tile(src, (1, 1, Q_BLOCK))[:, :, :Q_BLOCK * (period - 1)]
    skew = tiled.reshape(N_LIVE_TYPES, n_heads, Q_BLOCK, period - 1)[..., :Q_BLOCK]
    key_chunk = np.arange(Q_BLOCK)[:, None] // CHUNK
    qry_chunk = np.arange(Q_BLOCK)[None, :] // CHUNK
    lag = np.stack([2 * (t - (N_LIVE_TYPES - 1)) + key_chunk - qry_chunk for t in range(N_LIVE_TYPES)])
    valid = ((lag >= -LEFT_CHUNKS) & (lag <= 0))[:, None]
    live = jnp.where(valid, skew * LOG2E, NEG_INF)
    tab = jnp.concatenate([live, jnp.full((1,) + live.shape[1:], NEG_INF, F32)], axis=0)
    tab = tab.reshape(N_TYPES, n_heads // HEADS_PER_SLAB, HEADS_PER_SLAB, Q_BLOCK, Q_BLOCK)
    return tab.transpose(1, 0, 2, 3, 4)


def kernel(x, c, w_ada, b_ada, g_norm, w_ffn1_in, w_ffn1_out, w_in, q_norm_g, k_norm_g, rel_bias,
           w_conv, b_conv, g_attn_out, g_conv_out, w_out, w_ffn2_in, w_ffn2_out, g_final):
    bsz, seq, d = x.shape
    depth = w_ada.shape[0]
    da = g_attn_out.shape[1]
    dc = g_conv_out.shape[1]
    n_heads = da // HEAD_DIM
    assert seq % TM_PROJ == 0 and seq % TM_FFN == 0 and seq % GROUP_Q == 0
    assert da % LANES == 0 and w_in.shape[2] == 3 * da + 3 * dc
    row = lambda a: a.reshape(1, -1)
    for l in range(depth):
        mod = _mod(c, w_ada[l], b_ada[l]).reshape(bsz, 3 * N_SUB, 1, d)
        wgu1, wd1 = w_ffn1_in[l].astype(BF16), w_ffn1_out[l].astype(BF16)
        wgu2, wd2 = w_ffn2_in[l].astype(BF16), w_ffn2_out[l].astype(BF16)
        x = _ffn(x, mod, row(g_norm[l, 0]), wgu1, wd1)
        wvt = w_in[l][:, 2 * da:3 * da].T.astype(BF16)
        q, k, vt, yb = _in_proj(x, mod, row(g_norm[l, 1]), w_in[l].astype(BF16), wvt,
                                row(jnp.tile(q_norm_g[l], n_heads)), row(jnp.tile(k_norm_g[l], n_heads)),
                                w_conv[l], row(b_conv[l]), row(g_conv_out[l]))
        o = _attention(q, k, vt, _bias_tables(rel_bias[l]))
        x = _mix_ffn(o, yb, x, mod, row(g_attn_out[l]), w_out[l].astype(BF16),
                     row(g_norm[l, 2]), wgu2, wd2, row(g_final[l]))
    return x
```

```python
import math

import numpy as np
import jax
import jax.numpy as jnp
from jax import lax
from jax.experimental import pallas as pl
from jax.experimental.pallas import tpu as pltpu

F32 = jnp.float32
BF16 = jnp.bfloat16

CHUNK = 64
LEFT_CHUNKS = 8
HEAD_DIM = 64
CONV_WIDTH = 3
MAX_REL = 128
N_SUB = 3
FFN_RES = 0.5
EPS = 1e-6
NEG_INF = -1e30
LOG2E = 1.4426950408889634

LANES = 128
SUBLANES = 8
BF16_SUBLANES = 16
MXU_COL = 256

HEADS_PER_SLAB = LANES // HEAD_DIM
Q_BLOCK = LANES
GROUP_BLOCKS = 1
GROUP_Q = GROUP_BLOCKS * Q_BLOCK
GROUP_COLS = GROUP_BLOCKS * HEADS_PER_SLAB * Q_BLOCK
LEFT_KEYS = LEFT_CHUNKS * CHUNK
GROUP_BAND = LEFT_KEYS + GROUP_Q
GROUP_KEY_BLOCKS = GROUP_BAND // LANES
HALF_BAND_ALIGN = math.gcd(LANES, GROUP_BAND // 2)
N_START_GROUPS = LEFT_KEYS // GROUP_Q
N_LIVE_TYPES = LEFT_KEYS // Q_BLOCK + 1
N_TYPES = N_LIVE_TYPES + 1
FF_CHUNK = MXU_COL
TM_FFN = 1024
FFN_SPLIT = 4
TM_PROJ = 1024
PROJ_SPLIT = 4
PIPE_DIST = 8
ATTN_SLOTS = 2 * PIPE_DIST
VMEM_LIMIT = 56 * 1024 * 1024


def _resident(block_shape, index_map):
    return pl.BlockSpec(block_shape, index_map, pipeline_mode=pl.Buffered(1))


def _params(n_axes):
    return pltpu.CompilerParams(
        dimension_semantics=("arbitrary",) * n_axes,
        vmem_limit_bytes=VMEM_LIMIT)


def _rms(x, g):
    ms = jnp.mean(x * x, axis=-1, keepdims=True)
    return x * lax.rsqrt(ms + EPS) * g


def _adaln(x, g, scale, shift):
    return _rms(x, g) * (1.0 + scale) + shift


def _mod_kernel(c_ref, w_ref, b_ref, o_ref):
    c = c_ref[...]
    sc = c * (1.0 / (1.0 + jnp.exp(-c)))
    o_ref[...] = jnp.dot(sc.astype(BF16), w_ref[...].astype(BF16),
                         preferred_element_type=F32) + b_ref[...]


def _mod(c, w_ada, b_ada):
    bsz, d = c.shape
    n = w_ada.shape[1]
    tn = 1024
    return pl.pallas_call(
        _mod_kernel,
        out_shape=jax.ShapeDtypeStruct((bsz, n), F32),
        grid=(n // tn,),
        in_specs=[pl.BlockSpec((bsz, d), lambda j: (0, 0)),
                  pl.BlockSpec((d, tn), lambda j: (0, j)),
                  pl.BlockSpec((1, tn), lambda j: (0, j))],
        out_specs=pl.BlockSpec((bsz, tn), lambda j: (0, j)),
        compiler_params=_params(1),
        name="adaln_mod",
    )(c, w_ada, b_ada.reshape(1, n))


def _swiglu_residual(xs, shift, scale, gate, gn_ref, wgu_ref, wd_ref, h_ref, a_ref):
    rows = xs[0].shape[0]
    blk = [slice(i * rows, (i + 1) * rows) for i in range(len(xs))]
    for r, x in zip(blk, xs):
        h_ref[r, :] = _adaln(x, gn_ref[...], scale, shift).astype(BF16)
    d_ff = wd_ref.shape[0]
    tf = FF_CHUNK
    for r in blk:
        for c in range(d_ff // tf):
            w_c = jnp.concatenate([wgu_ref[:, c * tf:(c + 1) * tf],
                                   wgu_ref[:, d_ff + c * tf:d_ff + (c + 1) * tf]], axis=1)
            gu = jnp.dot(h_ref[r, :], w_c, preferred_element_type=F32)
            g = gu[:, :tf]
            u = gu[:, tf:]
            a_ref[r, c * tf:(c + 1) * tf] = (g * (1.0 / (1.0 + jnp.exp(-g))) * u).astype(BF16)
    outs = []
    for r, x in zip(blk, xs):
        acc = jnp.dot(a_ref[r, :], wd_ref[...], preferred_element_type=F32)
        outs.append(x + (FFN_RES * (1.0 + gate)) * acc)
    return outs


def _row_blocks(n_rows):
    rows = n_rows // FFN_SPLIT
    return [slice(i * rows, (i + 1) * rows) for i in range(FFN_SPLIT)]


def _ffn_kernel(x_ref, mod_ref, gn_ref, wgu_ref, wd_ref, o_ref, h_ref, a_ref):
    blk = _row_blocks(x_ref.shape[1])
    ys = _swiglu_residual([x_ref[0, r, :] for r in blk], mod_ref[0, 0], mod_ref[0, 1], mod_ref[0, 2],
                          gn_ref, wgu_ref, wd_ref, h_ref, a_ref)
    for r, y in zip(blk, ys):
        o_ref[0, r, :] = y


def _mix_ffn_kernel(o_ref, yb_ref, x_ref, mod_ref, ga_ref, wo_ref, gn_ref, wgu_ref, wd_ref, gf_ref,
                    out_ref, h_ref, a_ref):
    n_slab = o_ref.shape[1]
    da = n_slab * LANES
    blk = _row_blocks(x_ref.shape[1])
    xs = []
    for r in blk:
        ya = jnp.concatenate([o_ref[0, p, r, :] for p in range(n_slab)], axis=1).astype(F32)
        yan = _rms(ya, ga_ref[...]).astype(BF16)
        y = (jnp.dot(yan, wo_ref[0:da, :], preferred_element_type=F32)
             + jnp.dot(yb_ref[0, r, :], wo_ref[da:, :], preferred_element_type=F32))
        xs.append(x_ref[0, r, :] + (1.0 + mod_ref[0, 5]) * y)
    ys = _swiglu_residual(xs, mod_ref[0, 6], mod_ref[0, 7], mod_ref[0, 8],
                          gn_ref, wgu_ref, wd_ref, h_ref, a_ref)
    for r, y in zip(blk, ys):
        out_ref[0, r, :] = _rms(y, gf_ref[...])


def _ffn(x, mod, gn, wgu, wd):
    bsz, seq, d = x.shape
    tm = TM_FFN
    const2 = lambda b, i: (0, 0)
    return pl.pallas_call(
        _ffn_kernel,
        out_shape=jax.ShapeDtypeStruct((bsz, seq, d), F32),
        grid=(bsz, seq // tm),
        in_specs=[pl.BlockSpec((1, tm, d), lambda b, i: (b, i, 0)),
                  pl.BlockSpec((1, 3 * N_SUB, 1, d), lambda b, i: (b, 0, 0, 0)),
                  _resident((1, d), const2),
                  _resident(wgu.shape, const2),
                  _resident(wd.shape, const2)],
        out_specs=pl.BlockSpec((1, tm, d), lambda b, i: (b, i, 0)),
        scratch_shapes=[pltpu.VMEM((tm, d), BF16),
                        pltpu.VMEM((tm, wd.shape[0]), BF16)],
        compiler_params=_params(2),
        name="ffn",
    )(x, mod, gn, wgu, wd)


def _mix_ffn(o, yb, x, mod, ga, w_out, gn, wgu, wd, gf):
    bsz, seq, d = x.shape
    tm = TM_FFN
    n_slab = o.shape[1]
    dc = yb.shape[2]
    const2 = lambda b, i: (0, 0)
    return pl.pallas_call(
        _mix_ffn_kernel,
        out_shape=jax.ShapeDtypeStruct((bsz, seq, d), F32),
        grid=(bsz, seq // tm),
        in_specs=[pl.BlockSpec((1, n_slab, tm, LANES), lambda b, i: (b, 0, i, 0)),
                  pl.BlockSpec((1, tm, dc), lambda b, i: (b, i, 0)),
                  pl.BlockSpec((1, tm, d), lambda b, i: (b, i, 0)),
                  pl.BlockSpec((1, 3 * N_SUB, 1, d), lambda b, i: (b, 0, 0, 0)),
                  _resident((1, n_slab * LANES), const2),
                  _resident(w_out.shape, const2),
                  _resident((1, d), const2),
                  _resident(wgu.shape, const2),
                  _resident(wd.shape, const2),
                  _resident((1, d), const2)],
        out_specs=pl.BlockSpec((1, tm, d), lambda b, i: (b, i, 0)),
        scratch_shapes=[pltpu.VMEM((tm, d), BF16),
                        pltpu.VMEM((tm, wd.shape[0]), BF16)],
        compiler_params=_params(2),
        name="mix_ffn",
    )(o, yb, x, mod, ga, w_out, gn, wgu, wd, gf)


def _inproj_kernel(x_ref, mod_ref, gn_ref, w_ref, wvt_ref, qg_ref, kg_ref, wc_ref, bc_ref,
                   gc_ref, q_ref, k_ref, vt_ref, yb_ref, h_ref, u_ref):
    i = pl.program_id(1)
    tm = x_ref.shape[1]
    da = qg_ref.shape[1]
    dc = gc_ref.shape[1]
    n_slab = da // LANES
    rows = tm // PROJ_SPLIT
    lane = lax.broadcasted_iota(jnp.int32, (1, LANES), 1)

    def head_sumsq(t):
        slabs = []
        for p in range(n_slab):
            sq = t[:, p * LANES:(p + 1) * LANES]
            sq = sq * sq
            acc = None
            for hd in range(HEADS_PER_SLAB):
                mine = (lane >= hd * HEAD_DIM) & (lane < (hd + 1) * HEAD_DIM)
                tot = jnp.sum(jnp.where(mine, sq, 0.0), axis=-1, keepdims=True)
                acc = jnp.where(mine, tot, 0.0 if acc is None else acc)
            slabs.append(acc)
        return jnp.concatenate(slabs, axis=1)

    def head_norm(t, ss, g):
        return t * lax.rsqrt(ss * (1.0 / HEAD_DIM) + EPS) * g

    @pl.when(i == 0)
    def _():
        u_ref[0:SUBLANES, :] = jnp.zeros((SUBLANES, dc), F32)

    for b in range(PROJ_SPLIT):
        r = slice(b * rows, (b + 1) * rows)
        h_ref[r, :] = _adaln(x_ref[0, r, :], gn_ref[...], mod_ref[0, 4], mod_ref[0, 3]).astype(BF16)

        def proj(lo, width):
            return jnp.dot(h_ref[r, :], w_ref[:, lo:lo + width], preferred_element_type=F32)

        tq = proj(0, da)
        tk = proj(da, da)
        qn = head_norm(tq, head_sumsq(tq), qg_ref[...]) * (HEAD_DIM ** -0.5 * LOG2E)
        kn = head_norm(tk, head_sumsq(tk), kg_ref[...])
        for p in range(n_slab):
            q_ref[0, p, r, :] = qn[:, p * LANES:(p + 1) * LANES].astype(BF16)
            k_ref[0, p, r, :] = kn[:, p * LANES:(p + 1) * LANES].astype(BF16)

        xc = proj(3 * da, dc)
        gate_c = proj(3 * da + 2 * dc, dc)
        gate_b = proj(3 * da + dc, dc)
        u = gate_c * xc
        base = SUBLANES + b * rows
        u_ref[base:base + rows, :] = u
        conv = (wc_ref[0:1, :] * u_ref[base - 2:base - 2 + rows, :]
                + wc_ref[1:2, :] * u_ref[base - 1:base - 1 + rows, :]
                + wc_ref[2:3, :] * u + bc_ref[...])
        yb_ref[0, r, :] = _rms(gate_b * conv, gc_ref[...]).astype(BF16)

        vt = lax.dot_general(wvt_ref[...], h_ref[r, :], (((1,), (1,)), ((), ())),
                             preferred_element_type=F32)
        for p in range(n_slab):
            for t in range(rows // LANES):
                vt_ref[0, p, b * (rows // LANES) + t] = (
                    vt[p * LANES:(p + 1) * LANES, t * LANES:(t + 1) * LANES].astype(BF16))

    u_ref[0:SUBLANES, :] = u_ref[tm:tm + SUBLANES, :]


def _in_proj(x, mod, gn, w_in, wvt, qg, kg, wc, bc, gc):
    bsz, seq, d = x.shape
    tm = TM_PROJ
    da = qg.shape[1]
    dc = gc.shape[1]
    n_slab = da // LANES
    slab = jax.ShapeDtypeStruct((bsz, n_slab, seq, LANES), BF16)
    slab_spec = pl.BlockSpec((1, n_slab, tm, LANES), lambda b, i: (b, 0, i, 0))
    vt_shape = jax.ShapeDtypeStruct((bsz, n_slab, seq // LANES, LANES, LANES), BF16)
    vt_spec = pl.BlockSpec((1, n_slab, tm // LANES, LANES, LANES), lambda b, i: (b, 0, i, 0, 0))
    const = lambda b, i: (0, 0)
    return pl.pallas_call(
        _inproj_kernel,
        out_shape=(slab, slab, vt_shape, jax.ShapeDtypeStruct((bsz, seq, dc), BF16)),
        grid=(bsz, seq // tm),
        in_specs=[pl.BlockSpec((1, tm, d), lambda b, i: (b, i, 0)),
                  pl.BlockSpec((1, 3 * N_SUB, 1, d), lambda b, i: (b, 0, 0, 0)),
                  _resident((1, d), const),
                  _resident(w_in.shape, const),
                  _resident(wvt.shape, const),
                  _resident((1, da), const),
                  _resident((1, da), const),
                  _resident((CONV_WIDTH, dc), const),
                  _resident((1, dc), const),
                  _resident((1, dc), const)],
        out_specs=(slab_spec, slab_spec, vt_spec,
                   pl.BlockSpec((1, tm, dc), lambda b, i: (b, i, 0))),
        scratch_shapes=[pltpu.VMEM((tm, d), BF16),
                        pltpu.VMEM((tm + SUBLANES, dc), F32)],
        compiler_params=_params(2),
        name="in_proj",
    )(x, mod, gn, w_in, wvt, qg, kg, wc, bc, gc)


def _attn_kernel(q_ref, k_ref, vt_ref, tab_ref, o_ref, *scratch):
    n_grp = q_ref.shape[2] // GROUP_Q
    st_buf = scratch[:ATTN_SLOTS]
    m_buf = scratch[ATTN_SLOTS:2 * ATTN_SLOTS]
    p_buf = scratch[2 * ATTN_SLOTS:]
    ones_rows = jnp.ones((BF16_SUBLANES, GROUP_BAND), BF16)
    lane_first = lax.broadcasted_iota(jnp.int32, (Q_BLOCK, LANES), 1) < HEAD_DIM
    row_first = lax.broadcasted_iota(jnp.int32, (LANES, Q_BLOCK), 0) < HEAD_DIM

    def first_key_block(g):
        return jnp.maximum(g - N_START_GROUPS, 0) * GROUP_BLOCKS

    def q_rows(g, blk):
        return pl.ds(pl.multiple_of(g * GROUP_Q + blk * Q_BLOCK, Q_BLOCK), Q_BLOCK)

    def table(g):
        lag0 = GROUP_BLOCKS * jnp.minimum(g, N_START_GROUPS) - (N_LIVE_TYPES - 1)
        rows = []
        for cb in range(GROUP_KEY_BLOCKS):
            cols = []
            for blk in range(GROUP_BLOCKS):
                t = cb - blk - lag0
                t = jnp.where((t >= 0) & (t < N_LIVE_TYPES), t, N_LIVE_TYPES)
                cols += [tab_ref[0, t, h] for h in range(HEADS_PER_SLAB)]
            rows.append(jnp.concatenate(cols, axis=1))
        return jnp.concatenate(rows, axis=0)

    def scores(g, slot):
        parts = []
        for blk in range(GROUP_BLOCKS):
            q = q_ref[0, 0, q_rows(g, blk), :]
            zero = jnp.zeros_like(q)
            parts += [jnp.where(lane_first, q, zero), jnp.where(lane_first, zero, q)]
        q_all = jnp.concatenate(parts, axis=0)
        key0 = first_key_block(g) * LANES
        half = GROUP_BAND // 2
        st = jnp.concatenate([
            lax.dot_general(k_ref[0, 0, pl.ds(pl.multiple_of(key0 + h * half, HALF_BAND_ALIGN), half), :],
                            q_all, (((1,), (1,)), ((), ())), preferred_element_type=F32)
            for h in range(2)], axis=0)
        st = st + table(g)
        st_buf[slot][...] = st
        m_buf[slot][...] = jnp.max(st, axis=0, keepdims=True)

    def softmax(slot):
        p_buf[slot][...] = jnp.exp2(st_buf[slot][...] - m_buf[slot][...]).astype(BF16)

    def output(g, slot):
        kb0 = first_key_block(g)
        vt = jnp.concatenate([vt_ref[0, 0, kb0 + t] for t in range(GROUP_KEY_BLOCKS)], axis=1)
        ot = jnp.dot(jnp.concatenate([ones_rows, vt], axis=0), p_buf[slot][...],
                     preferred_element_type=F32)
        ot = ot[BF16_SUBLANES:] * (1.0 / ot[0:1])
        for blk in range(GROUP_BLOCKS):
            c0 = blk * HEADS_PER_SLAB * Q_BLOCK
            o_pair = jnp.where(row_first, ot[:, c0:c0 + Q_BLOCK], ot[:, c0 + Q_BLOCK:c0 + 2 * Q_BLOCK])
            o_ref[0, 0, q_rows(g, blk), :] = o_pair.T.astype(BF16)

    dist = PIPE_DIST
    for g in range(2 * dist):
        scores(g, g % ATTN_SLOTS)
    for g in range(dist):
        softmax(g % ATTN_SLOTS)

    def body(i, carry):
        for t in range(ATTN_SLOTS):
            g = ATTN_SLOTS * i + t
            output(g, t)
            scores(g + 2 * dist, t)
            softmax((t + dist) % ATTN_SLOTS)
        return carry

    lax.fori_loop(0, (n_grp - 2 * dist) // ATTN_SLOTS, body, 0)
    for g in range(n_grp - 2 * dist, n_grp):
        output(g, g % ATTN_SLOTS)
        if g + dist < n_grp:
            softmax((g + dist) % ATTN_SLOTS)


def _attention(q, k, vt, tab):
    bsz, n_slab, seq, _ = q.shape
    assert (seq // GROUP_Q - 2 * PIPE_DIST) % ATTN_SLOTS == 0
    full = pl.BlockSpec((1, 1, seq, LANES), lambda p, b: (b, p, 0, 0))
    scores = pltpu.VMEM((GROUP_BAND, GROUP_COLS), F32)
    probs = pltpu.VMEM((GROUP_BAND, GROUP_COLS), BF16)
    stat = pltpu.VMEM((1, GROUP_COLS), F32)
    return pl.pallas_call(
        _attn_kernel,
        out_shape=jax.ShapeDtypeStruct(q.shape, BF16),
        grid=(n_slab, bsz),
        in_specs=[full, full,
                  pl.BlockSpec((1, 1) + vt.shape[2:], lambda p, b: (b, p, 0, 0, 0)),
                  pl.BlockSpec((1,) + tab.shape[1:], lambda p, b: (p, 0, 0, 0, 0))],
        out_specs=full,
        scratch_shapes=[scores] * ATTN_SLOTS + [stat] * ATTN_SLOTS + [probs] * ATTN_SLOTS,
        compiler_params=_params(2),
        name="band_attn",
    )(q, k, vt, tab)


def _bias_tables(rel_bias):
    n_heads = rel_bias.shape[0]
    pad_left = LEFT_KEYS
    edge = jnp.concatenate([jnp.repeat(rel_bias[:, :1], pad_left, axis=1), rel_bias], axis=1)
    rev = edge[:, ::-1]
    period = 2 * Q_BLOCK
    rows = []
    for t in range(N_LIVE_TYPES):
        base = pad_left + MAX_REL + Q_BLOCK * (t - (N_LIVE_TYPES - 1))
        q0 = edge.shape[1] - 1 - base
        rows.append(jnp.concatenate([rev[:, q0:q0 + Q_BLOCK], rev[:, q0 - Q_BLOCK:q0]], axis=1))
    src = jnp.stack(rows, axis=0)
    tiled = jnp.tile(src, (1, 1, Q_BLOCK))[:, :, :Q_BLOCK * (period - 1)]
    skew = tiled.reshape(N_LIVE_TYPES, n_heads, Q_BLOCK, period - 1)[..., :Q_BLOCK]
    key_chunk = np.arange(Q_BLOCK)[:, None] // CHUNK
    qry_chunk = np.arange(Q_BLOCK)[None, :] // CHUNK
    lag = np.stack([2 * (t - (N_LIVE_TYPES - 1)) + key_chunk - qry_chunk for t in range(N_LIVE_TYPES)])
    valid = ((lag >= -LEFT_CHUNKS) & (lag <= 0))[:, None]
    live = jnp.where(valid, skew * LOG2E, NEG_INF)
    tab = jnp.concatenate([live, jnp.full((1,) + live.shape[1:], NEG_INF, F32)], axis=0)
    tab = tab.reshape(N_TYPES, n_heads // HEADS_PER_SLAB, HEADS_PER_SLAB, Q_BLOCK, Q_BLOCK)
    return tab.transpose(1, 0, 2, 3, 4)


def kernel(x, c, w_ada, b_ada, g_norm, w_ffn1_in, w_ffn1_out, w_in, q_norm_g, k_norm_g, rel_bias,
           w_conv, b_conv, g_attn_out, g_conv_out, w_out, w_ffn2_in, w_ffn2_out, g_final):
    bsz, seq, d = x.shape
    depth = w_ada.shape[0]
    da = g_attn_out.shape[1]
    dc = g_conv_out.shape[1]
    n_heads = da // HEAD_DIM
    assert seq % TM_PROJ == 0 and seq % TM_FFN == 0 and seq % GROUP_Q == 0
    assert da % LANES == 0 and w_in.shape[2] == 3 * da + 3 * dc
    row = lambda a: a.reshape(1, -1)
    for l in range(depth):
        mod = _mod(c, w_ada[l], b_ada[l]).reshape(bsz, 3 * N_SUB, 1, d)
        wgu1, wd1 = w_ffn1_in[l].astype(BF16), w_ffn1_out[l].astype(BF16)
        wgu2, wd2 = w_ffn2_in[l].astype(BF16), w_ffn2_out[l].astype(BF16)
        x = _ffn(x, mod, row(g_norm[l, 0]), wgu1, wd1)
        wvt = w_in[l][:, 2 * da:3 * da].T.astype(BF16)
        q, k, vt, yb = _in_proj(x, mod, row(g_norm[l, 1]), w_in[l].astype(BF16), wvt,
                                row(jnp.tile(q_norm_g[l], n_heads)), row(jnp.tile(k_norm_g[l], n_heads)),
                                w_conv[l], row(b_conv[l]), row(g_conv_out[l]))
        o = _attention(q, k, vt, _bias_tables(rel_bias[l]))
        x = _mix_ffn(o, yb, x, mod, row(g_attn_out[l]), w_out[l].astype(BF16),
                     row(g_norm[l, 2]), wgu2, wd2, row(g_final[l]))
    return x
```

```python
import math

import jax
import jax.numpy as jnp
from jax import lax
from jax.experimental import pallas as pl
from jax.experimental.pallas import tpu as pltpu

F32 = jnp.float32
BF16 = jnp.bfloat16

CHUNK = 64
LEFT_CHUNKS = 8
HEAD_DIM = 64
CONV_WIDTH = 3
MAX_REL = 128
N_SUB = 3
FFN_RES = 0.5
EPS = 1e-6
NEG_INF = -1e30
LOG2E = 1.4426950408889634

LANES = 128
SUBLANES = 8
BF16_SUBLANES = 16
MXU_COL = 256

HEADS_PER_SLAB = LANES // HEAD_DIM
Q_BLOCK = LANES
GROUP_BLOCKS = 1
GROUP_Q = GROUP_BLOCKS * Q_BLOCK
GROUP_COLS = GROUP_BLOCKS * HEADS_PER_SLAB * Q_BLOCK
LEFT_KEYS = LEFT_CHUNKS * CHUNK
GROUP_BAND = LEFT_KEYS + GROUP_Q
GROUP_KEY_BLOCKS = GROUP_BAND // LANES
HALF_BAND_ALIGN = math.gcd(LANES, GROUP_BAND // 2)
N_START_GROUPS = LEFT_KEYS // GROUP_Q
N_LIVE_TYPES = LEFT_KEYS // Q_BLOCK + 1
N_TYPES = N_LIVE_TYPES + 1
FF_CHUNK = MXU_COL
TM_FFN = 1024
FFN_SPLIT = 4
TM_PROJ = 1024
PROJ_SPLIT = 4
PIPE_DIST = 8
ATTN_SLOTS = 2 * PIPE_DIST
VMEM_LIMIT = 56 * 1024 * 1024


def _resident(block_shape, index_map):
    return pl.BlockSpec(block_shape, index_map, pipeline_mode=pl.Buffered(1))


def _params(n_axes):
    return pltpu.CompilerParams(
        dimension_semantics=("arbitrary",) * n_axes,
        vmem_limit_bytes=VMEM_LIMIT)


def _rms(x, g):
    ms = jnp.mean(x * x, axis=-1, keepdims=True)
    return x * lax.rsqrt(ms + EPS) * g


def _adaln(x, g, scale, shift):
    return _rms(x, g) * (1.0 + scale) + shift


def _mod_kernel(c_ref, w_ref, b_ref, o_ref):
    c = c_ref[...]
    sc = c * (1.0 / (1.0 + jnp.exp(-c)))
    o_ref[...] = jnp.dot(sc.astype(BF16), w_ref[...].astype(BF16),
                         preferred_element_type=F32) + b_ref[...]


def _mod(c, w_ada, b_ada):
    bsz, d = c.shape
    n = w_ada.shape[1]
    tn = 1024
    return pl.pallas_call(
        _mod_kernel,
        out_shape=jax.ShapeDtypeStruct((bsz, n), F32),
        grid=(n // tn,),
        in_specs=[pl.BlockSpec((bsz, d), lambda j: (0, 0)),
                  pl.BlockSpec((d, tn), lambda j: (0, j)),
                  pl.BlockSpec((1, tn), lambda j: (0, j))],
        out_specs=pl.BlockSpec((bsz, tn), lambda j: (0, j)),
        compiler_params=_params(1),
        name="adaln_mod",
    )(c, w_ada, b_ada.reshape(1, n))


def _swiglu_residual(xs, shift, scale, gate, gn_ref, wgu_ref, wd_ref, h_ref, a_ref):
    rows = xs[0].shape[0]
    blk = [slice(i * rows, (i + 1) * rows) for i in range(len(xs))]
    for r, x in zip(blk, xs):
        h_ref[r, :] = _adaln(x, gn_ref[...], scale, shift).astype(BF16)
    d_ff = wd_ref.shape[0]
    tf = FF_CHUNK
    for r in blk:
        for c in range(d_ff // tf):
            w_c = jnp.concatenate([wgu_ref[:, c * tf:(c + 1) * tf],
                                   wgu_ref[:, d_ff + c * tf:d_ff + (c + 1) * tf]], axis=1)
            gu = jnp.dot(h_ref[r, :], w_c, preferred_element_type=F32)
            g = gu[:, :tf]
            u = gu[:, tf:]
            a_ref[r, c * tf:(c + 1) * tf] = (g * (1.0 / (1.0 + jnp.exp(-g))) * u).astype(BF16)
    outs = []
    for r, x in zip(blk, xs):
        acc = jnp.dot(a_ref[r, :], wd_ref[...], preferred_element_type=F32)
        outs.append(x + (FFN_RES * (1.0 + gate)) * acc)
    return outs


def _row_blocks(n_rows):
    rows = n_rows // FFN_SPLIT
    return [slice(i * rows, (i + 1) * rows) for i in range(FFN_SPLIT)]


def _ffn_kernel(x_ref, mod_ref, gn_ref, wgu_ref, wd_ref, o_ref, h_ref, a_ref):
    blk = _row_blocks(x_ref.shape[1])
    ys = _swiglu_residual([x_ref[0, r, :] for r in blk], mod_ref[0, 0], mod_ref[0, 1], mod_ref[0, 2],
                          gn_ref, wgu_ref, wd_ref, h_ref, a_ref)
    for r, y in zip(blk, ys):
        o_ref[0, r, :] = y


def _mix_ffn_kernel(o_ref, yb_ref, x_ref, mod_ref, ga_ref, wo_ref, gn_ref, wgu_ref, wd_ref, gf_ref,
                    out_ref, h_ref, a_ref):
    n_slab = o_ref.shape[1]
    da = n_slab * LANES
    blk = _row_blocks(x_ref.shape[1])
    xs = []
    for r in blk:
        ya = jnp.concatenate([o_ref[0, p, r, :] for p in range(n_slab)], axis=1).astype(F32)
        yan = _rms(ya, ga_ref[...]).astype(BF16)
        y = (jnp.dot(yan, wo_ref[0:da, :], preferred_element_type=F32)
             + jnp.dot(yb_ref[0, r, :], wo_ref[da:, :], preferred_element_type=F32))
        xs.append(x_ref[0, r, :] + (1.0 + mod_ref[0, 5]) * y)
    ys = _swiglu_residual(xs, mod_ref[0, 6], mod_ref[0, 7], mod_ref[0, 8],
                          gn_ref, wgu_ref, wd_ref, h_ref, a_ref)
    for r, y in zip(blk, ys):
        out_ref[0, r, :] = _rms(y, gf_ref[...])


def _ffn(x, mod, gn, wgu, wd):
    bsz, seq, d = x.shape
    tm = TM_FFN
    const2 = lambda b, i: (0, 0)
    return pl.pallas_call(
        _ffn_kernel,
        out_shape=jax.ShapeDtypeStruct((bsz, seq, d), F32),
        grid=(bsz, seq // tm),
        in_specs=[pl.BlockSpec((1, tm, d), lambda b, i: (b, i, 0)),
                  pl.BlockSpec((1, 3 * N_SUB, 1, d), lambda b, i: (b, 0, 0, 0)),
                  _resident((1, d), const2),
                  _resident(wgu.shape, const2),
                  _resident(wd.shape, const2)],
        out_specs=pl.BlockSpec((1, tm, d), lambda b, i: (b, i, 0)),
        scratch_shapes=[pltpu.VMEM((tm, d), BF16),
                        pltpu.VMEM((tm, wd.shape[0]), BF16)],
        compiler_params=_params(2),
        name="ffn",
    )(x, mod, gn, wgu, wd)


def _mix_ffn(o, yb, x, mod, ga, w_out, gn, wgu, wd, gf):
    bsz, seq, d = x.shape
    tm = TM_FFN
    n_slab = o.shape[1]
    dc = yb.shape[2]
    const2 = lambda b, i: (0, 0)
    return pl.pallas_call(
        _mix_ffn_kernel,
        out_shape=jax.ShapeDtypeStruct((bsz, seq, d), F32),
        grid=(bsz, seq // tm),
        in_specs=[pl.BlockSpec((1, n_slab, tm, LANES), lambda b, i: (b, 0, i, 0)),
                  pl.BlockSpec((1, tm, dc), lambda b, i: (b, i, 0)),
                  pl.BlockSpec((1, tm, d), lambda b, i: (b, i, 0)),
                  pl.BlockSpec((1, 3 * N_SUB, 1, d), lambda b, i: (b, 0, 0, 0)),
                  _resident((1, n_slab * LANES), const2),
                  _resident(w_out.shape, const2),
                  _resident((1, d), const2),
                  _resident(wgu.shape, const2),
                  _resident(wd.shape, const2),
                  _resident((1, d), const2)],
        out_specs=pl.BlockSpec((1, tm, d), lambda b, i: (b, i, 0)),
        scratch_shapes=[pltpu.VMEM((tm, d), BF16),
                        pltpu.VMEM((tm, wd.shape[0]), BF16)],
        compiler_params=_params(2),
        name="mix_ffn",
    )(o, yb, x, mod, ga, w_out, gn, wgu, wd, gf)


def _inproj_kernel(x_ref, mod_ref, gn_ref, w_ref, wvt_ref, qg_ref, kg_ref, wc_ref, bc_ref,
                   gc_ref, q_ref, k_ref, vt_ref, yb_ref, h_ref, u_ref):
    i = pl.program_id(1)
    tm = x_ref.shape[1]
    da = qg_ref.shape[1]
    dc = gc_ref.shape[1]
    n_slab = da // LANES
    rows = tm // PROJ_SPLIT
    lane = lax.broadcasted_iota(jnp.int32, (1, LANES), 1)

    def head_sumsq(t):
        slabs = []
        for p in range(n_slab):
            sq = t[:, p * LANES:(p + 1) * LANES]
            sq = sq * sq
            acc = None
            for hd in range(HEADS_PER_SLAB):
                mine = (lane >= hd * HEAD_DIM) & (lane < (hd + 1) * HEAD_DIM)
                tot = jnp.sum(jnp.where(mine, sq, 0.0), axis=-1, keepdims=True)
                acc = jnp.where(mine, tot, 0.0 if acc is None else acc)
            slabs.append(acc)
        return jnp.concatenate(slabs, axis=1)

    def head_norm(t, ss, g):
        return t * lax.rsqrt(ss * (1.0 / HEAD_DIM) + EPS) * g

    @pl.when(i == 0)
    def _():
        u_ref[0:SUBLANES, :] = jnp.zeros((SUBLANES, dc), F32)

    for b in range(PROJ_SPLIT):
        r = slice(b * rows, (b + 1) * rows)
        h_ref[r, :] = _adaln(x_ref[0, r, :], gn_ref[...], mod_ref[0, 4], mod_ref[0, 3]).astype(BF16)

        def proj(lo, width):
            return jnp.dot(h_ref[r, :], w_ref[:, lo:lo + width], preferred_element_type=F32)

        tq = proj(0, da)
        tk = proj(da, da)
        qn = head_norm(tq, head_sumsq(tq), qg_ref[...]) * (HEAD_DIM ** -0.5 * LOG2E)
        kn = head_norm(tk, head_sumsq(tk), kg_ref[...])
        for p in range(n_slab):
            q_ref[0, p, r, :] = qn[:, p * LANES:(p + 1) * LANES].astype(BF16)
            k_ref[0, p, r, :] = kn[:, p * LANES:(p + 1) * LANES].astype(BF16)

        xc = proj(3 * da, dc)
        gate_c = proj(3 * da + 2 * dc, dc)
        gate_b = proj(3 * da + dc, dc)
        u = gate_c * xc
        base = SUBLANES + b * rows
        u_ref[base:base + rows, :] = u
        conv = (wc_ref[0:1, :] * u_ref[base - 2:base - 2 + rows, :]
                + wc_ref[1:2, :] * u_ref[base - 1:base - 1 + rows, :]
                + wc_ref[2:3, :] * u + bc_ref[...])
        yb_ref[0, r, :] = _rms(gate_b * conv, gc_ref[...]).astype(BF16)

        vt = lax.dot_general(wvt_ref[...], h_ref[r, :], (((1,), (1,)), ((), ())),
                             preferred_element_type=F32)
        for p in range(n_slab):
            for t in range(rows // LANES):
                vt_ref[0, p, b * (rows // LANES) + t] = (
                    vt[p * LANES:(p + 1) * LANES, t * LANES:(t + 1) * LANES].astype(BF16))

    u_ref[0:SUBLANES, :] = u_ref[tm:tm + SUBLANES, :]


def _in_proj(x, mod, gn, w_in, wvt, qg, kg, wc, bc, gc):
    bsz, seq, d = x.shape
    tm = TM_PROJ
    da = qg.shape[1]
    dc = gc.shape[1]
    n_slab = da // LANES
    slab = jax.ShapeDtypeStruct((bsz, n_slab, seq, LANES), BF16)
    slab_spec = pl.BlockSpec((1, n_slab, tm, LANES), lambda b, i: (b, 0, i, 0))
    vt_shape = jax.ShapeDtypeStruct((bsz, n_slab, seq // LANES, LANES, LANES), BF16)
    vt_spec = pl.BlockSpec((1, n_slab, tm // LANES, LANES, LANES), lambda b, i: (b, 0, i, 0, 0))
    const = lambda b, i: (0, 0)
    return pl.pallas_call(
        _inproj_kernel,
        out_shape=(slab, slab, vt_shape, jax.ShapeDtypeStruct((bsz, seq, dc), BF16)),
        grid=(bsz, seq // tm),
        in_specs=[pl.BlockSpec((1, tm, d), lambda b, i: (b, i, 0)),
                  pl.BlockSpec((1, 3 * N_SUB, 1, d), lambda b, i: (b, 0, 0, 0)),
                  _resident((1, d), const),
                  _resident(w_in.shape, const),
                  _resident(wvt.shape, const),
                  _resident((1, da), const),
                  _resident((1, da), const),
                  _resident((CONV_WIDTH, dc), const),
                  _resident((1, dc), const),
                  _resident((1, dc), const)],
        out_specs=(slab_spec, slab_spec, vt_spec,
                   pl.BlockSpec((1, tm, dc), lambda b, i: (b, i, 0))),
        scratch_shapes=[pltpu.VMEM((tm, d), BF16),
                        pltpu.VMEM((tm + SUBLANES, dc), F32)],
        compiler_params=_params(2),
        name="in_proj",
    )(x, mod, gn, w_in, wvt, qg, kg, wc, bc, gc)


def _attn_kernel(q_ref, k_ref, vt_ref, tab_ref, o_ref, *scratch):
    n_grp = q_ref.shape[2] // GROUP_Q
    st_buf = scratch[:ATTN_SLOTS]
    m_buf = scratch[ATTN_SLOTS:2 * ATTN_SLOTS]
    p_buf = scratch[2 * ATTN_SLOTS:]
    ones_rows = jnp.ones((BF16_SUBLANES, GROUP_BAND), BF16)
    lane_first = lax.broadcasted_iota(jnp.int32, (Q_BLOCK, LANES), 1) < HEAD_DIM
    row_first = lax.broadcasted_iota(jnp.int32, (LANES, Q_BLOCK), 0) < HEAD_DIM

    def first_key_block(g):
        return jnp.maximum(g - N_START_GROUPS, 0) * GROUP_BLOCKS

    def q_rows(g, blk):
        return pl.ds(pl.multiple_of(g * GROUP_Q + blk * Q_BLOCK, Q_BLOCK), Q_BLOCK)

    def table(g):
        lag0 = GROUP_BLOCKS * jnp.minimum(g, N_START_GROUPS) - (N_LIVE_TYPES - 1)
        rows = []
        for cb in range(GROUP_KEY_BLOCKS):
            cols = []
            for blk in range(GROUP_BLOCKS):
                t = cb - blk - lag0
                t = jnp.where((t >= 0) & (t < N_LIVE_TYPES), t, N_LIVE_TYPES)
                cols += [tab_ref[0, t, h] for h in range(HEADS_PER_SLAB)]
            rows.append(jnp.concatenate(cols, axis=1))
        return jnp.concatenate(rows, axis=0)

    def scores(g, slot):
        parts = []
        for blk in range(GROUP_BLOCKS):
            q = q_ref[0, 0, q_rows(g, blk), :]
            zero = jnp.zeros_like(q)
            parts += [jnp.where(lane_first, q, zero), jnp.where(lane_first, zero, q)]
        q_all = jnp.concatenate(parts, axis=0)
        key0 = first_key_block(g) * LANES
        half = GROUP_BAND // 2
        st = jnp.concatenate([
            lax.dot_general(k_ref[0, 0, pl.ds(pl.multiple_of(key0 + h * half, HALF_BAND_ALIGN), half), :],
                            q_all, (((1,), (1,)), ((), ())), preferred_element_type=F32)
            for h in range(2)], axis=0)
        st = st + table(g)
        st_buf[slot][...] = st
        m_buf[slot][...] = jnp.max(st, axis=0, keepdims=True)

    def softmax(slot):
        p_buf[slot][...] = jnp.exp2(st_buf[slot][...] - m_buf[slot][...]).astype(BF16)

    def output(g, slot):
        kb0 = first_key_block(g)
        vt = jnp.concatenate([vt_ref[0, 0, kb0 + t] for t in range(GROUP_KEY_BLOCKS)], axis=1)
        ot = jnp.dot(jnp.concatenate([ones_rows, vt], axis=0), p_buf[slot][...],
                     preferred_element_type=F32)
        ot = ot[BF16_SUBLANES:] * (1.0 / ot[0:1])
        for blk in range(GROUP_BLOCKS):
            c0 = blk * HEADS_PER_SLAB * Q_BLOCK
            o_pair = jnp.where(row_first, ot[:, c0:c0 + Q_BLOCK], ot[:, c0 + Q_BLOCK:c0 + 2 * Q_BLOCK])
            o_ref[0, 0, q_rows(g, blk), :] = o_pair.T.astype(BF16)

    dist = PIPE_DIST
    for g in range(2 * dist):
        scores(g, g % ATTN_SLOTS)
    for g in range(dist):
        softmax(g % ATTN_SLOTS)

    def body(i, carry):
        for t in range(ATTN_SLOTS):
            g = ATTN_SLOTS * i + t
            output(g, t)
            scores(g + 2 * dist, t)
            softmax((t + dist) % ATTN_SLOTS)
        return carry

    lax.fori_loop(0, (n_grp - 2 * dist) // ATTN_SLOTS, body, 0)
    for g in range(n_grp - 2 * dist, n_grp):
        output(g, g % ATTN_SLOTS)
        if g + dist < n_grp:
            softmax((g + dist) % ATTN_SLOTS)


def _attention(q, k, vt, tab):
    bsz, n_slab, seq, _ = q.shape
    assert (seq // GROUP_Q - 2 * PIPE_DIST) % ATTN_SLOTS == 0
    full = pl.BlockSpec((1, 1, seq, LANES), lambda p, b: (b, p, 0, 0))
    scores = pltpu.VMEM((GROUP_BAND, GROUP_COLS), F32)
    probs = pltpu.VMEM((GROUP_BAND, GROUP_COLS), BF16)
    stat = pltpu.VMEM((1, GROUP_COLS), F32)
    return pl.pallas_call(
        _attn_kernel,
        out_shape=jax.ShapeDtypeStruct(q.shape, BF16),
        grid=(n_slab, bsz),
        in_specs=[full, full,
                  pl.BlockSpec((1, 1) + vt.shape[2:], lambda p, b: (b, p, 0, 0, 0)),
                  pl.BlockSpec((1,) + tab.shape[1:], lambda p, b: (p, 0, 0, 0, 0))],
        out_specs=full,
        scratch_shapes=[scores] * ATTN_SLOTS + [stat] * ATTN_SLOTS + [probs] * ATTN_SLOTS,
        compiler_params=_params(2),
        name="band_attn",
    )(q, k, vt, tab)


def _table_kernel(src_ref, o_ref):
    n_pairs = o_ref.shape[0]
    key_chunk = lax.broadcasted_iota(jnp.int32, (Q_BLOCK, Q_BLOCK), 0) // CHUNK
    qry_chunk = lax.broadcasted_iota(jnp.int32, (Q_BLOCK, Q_BLOCK), 1) // CHUNK
    for t in range(N_LIVE_TYPES):
        lag = 2 * (t - (N_LIVE_TYPES - 1)) + key_chunk - qry_chunk
        valid = (lag >= -LEFT_CHUNKS) & (lag <= 0)
        for p in range(n_pairs):
            for hh in range(HEADS_PER_SLAB):
                row = t * n_pairs * HEADS_PER_SLAB + p * HEADS_PER_SLAB + hh
                rows = jnp.broadcast_to(src_ref[row:row + 1, :], (Q_BLOCK, 2 * Q_BLOCK))
                skew = pltpu.roll(rows, 0, 1, stride=1, stride_axis=0)[:, :Q_BLOCK]
                o_ref[p, t, hh] = jnp.where(valid, skew * LOG2E, NEG_INF)
    for p in range(n_pairs):
        for hh in range(HEADS_PER_SLAB):
            o_ref[p, N_LIVE_TYPES, hh] = jnp.full((Q_BLOCK, Q_BLOCK), NEG_INF, F32)


def _bias_tables(rel_bias):
    n_heads = rel_bias.shape[0]
    n_pairs = n_heads // HEADS_PER_SLAB
    pad_left = LEFT_KEYS
    edge = jnp.concatenate([jnp.repeat(rel_bias[:, :1], pad_left, axis=1), rel_bias], axis=1)
    rev = edge[:, ::-1]
    rows = []
    for t in range(N_LIVE_TYPES):
        base = pad_left + MAX_REL + Q_BLOCK * (t - (N_LIVE_TYPES - 1))
        q0 = edge.shape[1] - 1 - base
        rows.append(jnp.concatenate([rev[:, q0:q0 + Q_BLOCK], rev[:, q0 - Q_BLOCK:q0]], axis=1))
    src = jnp.stack(rows, axis=0).reshape(N_LIVE_TYPES * n_heads, 2 * Q_BLOCK)
    out_shape = (n_pairs, N_TYPES, HEADS_PER_SLAB, Q_BLOCK, Q_BLOCK)
    return pl.pallas_call(
        _table_kernel,
        out_shape=jax.ShapeDtypeStruct(out_shape, F32),
        grid=(1,),
        in_specs=[pl.BlockSpec(src.shape, lambda i: (0, 0))],
        out_specs=pl.BlockSpec(out_shape, lambda i: (0, 0, 0, 0, 0)),
        compiler_params=_params(1),
        name="score_tables",
    )(src)


def kernel(x, c, w_ada, b_ada, g_norm, w_ffn1_in, w_ffn1_out, w_in, q_norm_g, k_norm_g, rel_bias,
           w_conv, b_conv, g_attn_out, g_conv_out, w_out, w_ffn2_in, w_ffn2_out, g_final):
    bsz, seq, d = x.shape
    depth = w_ada.shape[0]
    da = g_attn_out.shape[1]
    dc = g_conv_out.shape[1]
    n_heads = da // HEAD_DIM
    assert seq % TM_PROJ == 0 and seq % TM_FFN == 0 and seq % GROUP_Q == 0
    assert da % LANES == 0 and w_in.shape[2] == 3 * da + 3 * dc
    row = lambda a: a.reshape(1, -1)
    for l in range(depth):
        mod = _mod(c, w_ada[l], b_ada[l]).reshape(bsz, 3 * N_SUB, 1, d)
        wgu1, wd1 = w_ffn1_in[l].astype(BF16), w_ffn1_out[l].astype(BF16)
        wgu2, wd2 = w_ffn2_in[l].astype(BF16), w_ffn2_out[l].astype(BF16)
        x = _ffn(x, mod, row(g_norm[l, 0]), wgu1, wd1)
        wvt = w_in[l][:, 2 * da:3 * da].T.astype(BF16)
        q, k, vt, yb = _in_proj(x, mod, row(g_norm[l, 1]), w_in[l].astype(BF16), wvt,
                                row(jnp.tile(q_norm_g[l], n_heads)), row(jnp.tile(k_norm_g[l], n_heads)),
                                w_conv[l], row(b_conv[l]), row(g_conv_out[l]))
        o = _attention(q, k, vt, _bias_tables(rel_bias[l]))
        x = _mix_ffn(o, yb, x, mod, row(g_attn_out[l]), w_out[l].astype(BF16),
                     row(g_norm[l, 2]), wgu2, wd2, row(g_final[l]))
    return x
```
